```python
import math
import jax, jax.numpy as jnp
from jax import lax
import numpy as np

D_MODEL = 1024
BATCH = 8
SEQ = 8192
DEPTH = 1
DEC_BATCH = 32
DEC_SEQ = 32
PAST_LEN = 1024

CHUNK = 64
Q_BLOCK = 128
N_MEM = 256
EPS = 1e-6
DA_HEADS = 4
DA_HEAD_DIM = D_MODEL // 16
DA_VDIM = 2 * DA_HEAD_DIM
DA_QK = DA_HEADS * 2 * DA_HEAD_DIM
DA_WIDTH = DA_HEADS * DA_VDIM
ROPE_THETA = 500000.0
ROT_DIM = DA_HEAD_DIM // 4
LRU_WIDTH = D_MODEL // 4
LRU_BLOCKS = 4
LRU_BLOCK = LRU_WIDTH // LRU_BLOCKS
CONV_W = 4
LRU_C = 8.0
MX_HEADS = 4
MX_HEAD_DIM = D_MODEL // 16
MX_WIDTH = MX_HEADS * MX_HEAD_DIM
MIX_WIDTH = DA_WIDTH + LRU_WIDTH + MX_WIDTH
IN_SIZES = (DA_QK, DA_QK, DA_WIDTH, DA_WIDTH, LRU_WIDTH, LRU_WIDTH, MX_WIDTH, MX_WIDTH)
IN_WIDTH = 2 * DA_QK + 2 * DA_WIDTH + 2 * LRU_WIDTH + 2 * MX_WIDTH

kernel_name = "hybrid_diffattn_rglru_memxattn_stream_step"


def rms_norm(x, g):
    xf = x.astype(jnp.float32)
    y = xf * lax.rsqrt(jnp.mean(xf * xf, axis=-1, keepdims=True) + EPS)
    return (y * g.astype(jnp.float32)).astype(x.dtype)


def rope(x, pos):
    half = ROT_DIM // 2
    inv = ROPE_THETA ** (-jnp.arange(0, ROT_DIM, 2, dtype=jnp.float32) / ROT_DIM)
    ang = pos.astype(jnp.float32)[:, None] * inv[None, :]
    shape = (1, pos.shape[0]) + (1,) * (x.ndim - 3) + (half,)
    cos = jnp.cos(ang).reshape(shape)
    sin = jnp.sin(ang).reshape(shape)
    xr = x[..., :ROT_DIM].astype(jnp.float32)
    x1, x2 = xr[..., :half], xr[..., half:]
    rot = jnp.concatenate([x1 * cos - x2 * sin, x2 * cos + x1 * sin], axis=-1).astype(x.dtype)
    return jnp.concatenate([rot, x[..., ROT_DIM:]], axis=-1)


def split_cols(t, sizes):
    idx, acc = [], 0
    for s in sizes[:-1]:
        acc += s
        idx.append(acc)
    return jnp.split(t, idx, axis=-1)


def diff_attend(q, k, v, q_pos, k_pos, lam):
    s = jnp.einsum('bqhcd,bkhcd->bhcqk', q, k).astype(jnp.float32) * (DA_HEAD_DIM ** -0.5)
    mask = (k_pos[None, :] // CHUNK) <= (q_pos[:, None] // CHUNK)
    s = jnp.where(mask[None, None, None], s, jnp.float32(-1e30))
    p = jax.nn.softmax(s, axis=-1)
    w = p[:, :, 0] - lam * p[:, :, 1]
    return jnp.einsum('bhqk,bkhe->bqhe', w.astype(v.dtype), v)


def lru_branch(xb, conv_buf, h0, conv_w, conv_b, w_a, b_a, w_x, b_x, lru_lambda):
    B, T, W = xb.shape
    xp = jnp.concatenate([conv_buf.astype(xb.dtype), xb], axis=1)
    xc = conv_b
    for j in range(CONV_W):
        xc = xc + xp[:, j:j + T] * conv_w[j]
    new_buf = xp[:, -(CONV_W - 1):]
    xbk = xc.reshape(B, T, LRU_BLOCKS, LRU_BLOCK)
    r = jax.nn.sigmoid(jnp.einsum('btni,nij->btnj', xbk, w_a).reshape(B, T, W) + b_a)
    i = jax.nn.sigmoid(jnp.einsum('btni,nij->btnj', xbk, w_x).reshape(B, T, W) + b_x)
    log_a = -LRU_C * r.astype(jnp.float32) * jax.nn.softplus(-lru_lambda.astype(jnp.float32))
    a = jnp.exp(log_a)
    b = jnp.sqrt(-jnp.expm1(2.0 * log_a)) * (i * xc).astype(jnp.float32)

    def combine(e, l):
        return (e[0] * l[0], l[0] * e[1] + l[1])

    a_cum, h = lax.associative_scan(combine, (a, b), axis=1)
    h = h + a_cum * h0.astype(jnp.float32)[:, None, :]
    return h.astype(xb.dtype), new_buf, h[:, -1].astype(xb.dtype)


def mem_kv(mem, mem_norm_g, w_mem_kv, mx_k_norm_g):
    B, N, _ = mem.shape
    kv = rms_norm(mem, mem_norm_g) @ w_mem_kv
    k, v = split_cols(kv, (MX_WIDTH, MX_WIDTH))
    k = rms_norm(k.reshape(B, N, MX_HEADS, MX_HEAD_DIM), mx_k_norm_g)
    return k, v.reshape(B, N, MX_HEADS, MX_HEAD_DIM)


def layer(x, q_pos, past_k, past_v, conv_buf, h0, m_k, m_v, p, lambda_init, blocked):
    B, T, _ = x.shape
    hn = rms_norm(x, p['norm_g'])
    proj = hn @ p['w_in']
    dq, dk, dv, dg, lx, lg, mq, mg = split_cols(proj, IN_SIZES)
    q = rope(rms_norm(dq.reshape(B, T, DA_HEADS, 2, DA_HEAD_DIM), p['da_q_norm_g']), q_pos)
    k_new = rope(rms_norm(dk.reshape(B, T, DA_HEADS, 2, DA_HEAD_DIM), p['da_k_norm_g']), q_pos)
    v_new = dv.reshape(B, T, DA_HEADS, DA_VDIM)
    if past_k is None:
        k_all, v_all, k_pos = k_new, v_new, q_pos
    else:
        k_all = jnp.concatenate([past_k.astype(k_new.dtype), k_new], axis=1)
        v_all = jnp.concatenate([past_v.astype(v_new.dtype), v_new], axis=1)
        k_pos = jnp.arange(past_k.shape[1] + T, dtype=jnp.int32)
    f32 = jnp.float32
    lam = (jnp.exp(jnp.sum(p['lambda_q1'].astype(f32) * p['lambda_k1'].astype(f32)))
           - jnp.exp(jnp.sum(p['lambda_q2'].astype(f32) * p['lambda_k2'].astype(f32))) + lambda_init)
    if blocked:
        nb = T // Q_BLOCK
        qb = q.reshape(B, nb, Q_BLOCK, DA_HEADS, 2, DA_HEAD_DIM).transpose(1, 0, 2, 3, 4, 5)
        pb = q_pos.reshape(nb, Q_BLOCK)
        ob = lax.map(lambda a: diff_attend(a[0], k_all, v_all, a[1], k_pos, lam), (qb, pb))
        o = ob.transpose(1, 0, 2, 3, 4).reshape(B, T, DA_HEADS, DA_VDIM)
    else:
        o = diff_attend(q, k_all, v_all, q_pos, k_pos, lam)
    o = rms_norm(o, p['da_subln_g']) * (1.0 - lambda_init)
    out_a = o.reshape(B, T, DA_WIDTH) * jax.nn.silu(dg)
    hl, new_buf, h_last = lru_branch(lx, conv_buf, h0, p['lru_conv_w'], p['lru_conv_b'], p['lru_w_a'],
                                     p['lru_b_a'], p['lru_w_x'], p['lru_b_x'], p['lru_lambda'])
    out_b = hl * jax.nn.silu(lg)
    qm = rms_norm(mq.reshape(B, T, MX_HEADS, MX_HEAD_DIM), p['mx_q_norm_g'])
    sm = jnp.einsum('bqhd,bkhd->bhqk', qm, m_k.astype(qm.dtype)).astype(f32) * (MX_HEAD_DIM ** -0.5)
    pm = jax.nn.softmax(sm, axis=-1)
    om = jnp.einsum('bhqk,bkhd->bqhd', pm.astype(qm.dtype), m_v.astype(qm.dtype)).reshape(B, T, MX_WIDTH)
    out_c = om * jax.nn.silu(mg)
    y = x + jnp.concatenate([out_a, out_b, out_c], axis=-1) @ p['w_out']
    return y, k_new, v_new, new_buf, h_last


def setup_inputs(seed: int = 0) -> dict:
    key = jax.random.key(seed)
    ks = jax.random.split(key, 40)
    n = lambda i, shape, s=1.0: jax.random.normal(ks[i], shape, jnp.float32) * s
    a8 = jax.random.uniform(ks[39], (DEPTH, LRU_WIDTH), jnp.float32, 0.9, 0.999)
    a0 = a8 ** (1.0 / LRU_C)
    return {
        'x_prompt': n(0, (BATCH, SEQ, D_MODEL)),
        'x_sample': n(1, (DEC_BATCH, DEC_SEQ, D_MODEL)),
        'mem_prompt': n(2, (BATCH, N_MEM, D_MODEL)),
        'cache_diff_k': n(3, (DEPTH, DEC_BATCH, PAST_LEN, DA_HEADS, 2, DA_HEAD_DIM)),
        'cache_diff_v': n(4, (DEPTH, DEC_BATCH, PAST_LEN, DA_HEADS, DA_VDIM)),
        'cache_mem_k': n(5, (DEPTH, DEC_BATCH, N_MEM, MX_HEADS, MX_HEAD_DIM)),
        'cache_mem_v': n(6, (DEPTH, DEC_BATCH, N_MEM, MX_HEADS, MX_HEAD_DIM)),
        'state_lru_conv': n(7, (DEPTH, DEC_BATCH, CONV_W - 1, LRU_WIDTH)),
        'state_lru_h': n(8, (DEPTH, DEC_BATCH, LRU_WIDTH)),
        'norm_g': 1.0 + n(9, (DEPTH, D_MODEL), 0.02),
        'w_in': n(10, (DEPTH, D_MODEL, IN_WIDTH), D_MODEL ** -0.5),
        'da_q_norm_g': 1.0 + n(11, (DEPTH, DA_HEAD_DIM), 0.02),
        'da_k_norm_g': 1.0 + n(12, (DEPTH, DA_HEAD_DIM), 0.02),
        'lambda_q1': n(13, (DEPTH, DA_HEAD_DIM), 0.1),
        'lambda_k1': n(14, (DEPTH, DA_HEAD_DIM), 0.1),
        'lambda_q2': n(15, (DEPTH, DA_HEAD_DIM), 0.1),
        'lambda_k2': n(16, (DEPTH, DA_HEAD_DIM), 0.1),
        'da_subln_g': 1.0 + n(17, (DEPTH, DA_VDIM), 0.02),
        'lru_conv_w': n(18, (DEPTH, CONV_W, LRU_WIDTH), CONV_W ** -0.5),
        'lru_conv_b': n(19, (DEPTH, LRU_WIDTH), 0.01),
        'lru_w_a': n(20, (DEPTH, LRU_BLOCKS, LRU_BLOCK, LRU_BLOCK), LRU_BLOCK ** -0.5),
        'lru_b_a': n(21, (DEPTH, LRU_WIDTH), 0.01),
        'lru_w_x': n(22, (DEPTH, LRU_BLOCKS, LRU_BLOCK, LRU_BLOCK), LRU_BLOCK ** -0.5),
        'lru_b_x': n(23, (DEPTH, LRU_WIDTH), 0.01),
        'lru_lambda': jnp.log(a0) - jnp.log1p(-a0),
        'mem_norm_g': 1.0 + n(24, (DEPTH, D_MODEL), 0.02),
        'w_mem_kv': n(25, (DEPTH, D_MODEL, 2 * MX_WIDTH), D_MODEL ** -0.5),
        'mx_q_norm_g': 1.0 + n(26, (DEPTH, MX_HEAD_DIM), 0.02),
        'mx_k_norm_g': 1.0 + n(27, (DEPTH, MX_HEAD_DIM), 0.02),
        'w_out': n(28, (DEPTH, MIX_WIDTH, D_MODEL), MIX_WIDTH ** -0.5),
    }


def reference(x_prompt, x_sample, mem_prompt, cache_diff_k, cache_diff_v, cache_mem_k, cache_mem_v,
              state_lru_conv, state_lru_h, norm_g, w_in, da_q_norm_g, da_k_norm_g, lambda_q1, lambda_k1,
              lambda_q2, lambda_k2, da_subln_g, lru_conv_w, lru_conv_b, lru_w_a, lru_b_a, lru_w_x, lru_b_x,
              lru_lambda, mem_norm_g, w_mem_kv, mx_q_norm_g, mx_k_norm_g, w_out):
    Bp, Tp, _ = x_prompt.shape
    Bs, Ts, _ = x_sample.shape
    pos_p = jnp.arange(Tp, dtype=jnp.int32)
    pos_s = cache_diff_k.shape[2] + jnp.arange(Ts, dtype=jnp.int32)
    yp, ys = x_prompt, x_sample
    kp_l, vp_l, mkp_l, mvp_l, cp_l, hp_l = [], [], [], [], [], []
    ks_l, vs_l, cs_l, hs_l = [], [], [], []
    for l in range(DEPTH):
        p = dict(norm_g=norm_g[l], w_in=w_in[l], da_q_norm_g=da_q_norm_g[l], da_k_norm_g=da_k_norm_g[l],
                 lambda_q1=lambda_q1[l], lambda_k1=lambda_k1[l], lambda_q2=lambda_q2[l], lambda_k2=lambda_k2[l],
                 da_subln_g=da_subln_g[l], lru_conv_w=lru_conv_w[l], lru_conv_b=lru_conv_b[l],
                 lru_w_a=lru_w_a[l], lru_b_a=lru_b_a[l], lru_w_x=lru_w_x[l], lru_b_x=lru_b_x[l],
                 lru_lambda=lru_lambda[l], mx_q_norm_g=mx_q_norm_g[l], w_out=w_out[l])
        lambda_init = 0.8 - 0.6 * math.exp(-0.3 * l)
        mk_p, mv_p = mem_kv(mem_prompt, mem_norm_g[l], w_mem_kv[l], mx_k_norm_g[l])
        zbuf = jnp.zeros((Bp, CONV_W - 1, LRU_WIDTH), yp.dtype)
        zh = jnp.zeros((Bp, LRU_WIDTH), yp.dtype)
        yp, k_p, v_p, c_p, h_p = layer(yp, pos_p, None, None, zbuf, zh, mk_p, mv_p, p, lambda_init,
                                       Tp > Q_BLOCK and Tp % Q_BLOCK == 0)
        ys, k_s, v_s, c_s, h_s = layer(ys, pos_s, cache_diff_k[l], cache_diff_v[l], state_lru_conv[l],
                                       state_lru_h[l], cache_mem_k[l], cache_mem_v[l], p, lambda_init, False)
        kp_l.append(k_p); vp_l.append(v_p); mkp_l.append(mk_p); mvp_l.append(mv_p)
        cp_l.append(c_p); hp_l.append(h_p)
        ks_l.append(k_s); vs_l.append(v_s); cs_l.append(c_s); hs_l.append(h_s)
    return (yp, ys,
            jnp.stack(kp_l), jnp.stack(vp_l), jnp.stack(mkp_l), jnp.stack(mvp_l),
            jnp.stack(cp_l), jnp.stack(hp_l),
            jnp.stack(ks_l), jnp.stack(vs_l), jnp.stack(cs_l), jnp.stack(hs_l))
```

```python
import functools
import math

import numpy as np
import jax
import jax.numpy as jnp
from jax import lax
from jax.experimental import pallas as pl
from jax.experimental.pallas import tpu as pltpu

D_MODEL = 1024
CHUNK = 64
EPS = 1e-6
DA_HEADS = 4
DA_HEAD_DIM = 64
DA_VDIM = 128
DA_QK = 512
DA_WIDTH = 512
ROPE_THETA = 500000.0
ROT_DIM = 16
LRU_WIDTH = 256
LRU_BLOCKS = 4
CONV_W = 4
LRU_C = 8.0
MX_HEADS = 4
MX_HEAD_DIM = 64
MX_WIDTH = 256
N_MEM = 256
LAMBDA_INIT = 0.8 - 0.6 * math.exp(-0.3 * 0)
NEG_BIG = -1e30

OFF_DQ, OFF_DK, OFF_DV, OFF_DG = 0, 512, 1024, 1536
OFF_LX, OFF_LG, OFF_MQ, OFF_MG = 2048, 2304, 2560, 2816
IN_WIDTH = 3072

VMEM_LIMIT_BYTES = 52 * 1024 * 1024
LANES = 128
CONV_PAD = 8

F32 = jnp.float32
BF16 = jnp.bfloat16


def _dot(a, b):
    return jnp.dot(a, b, preferred_element_type=F32)


def _dot_nt(a, b):
    return lax.dot_general(a, b, (((1,), (1,)), ((), ())), preferred_element_type=F32)


def _sigmoid(x):
    return 1.0 / (1.0 + jnp.exp(-x))


def _silu(x):
    return x * _sigmoid(x)


def _group_rms_scale(x, ones_bd, group):
    ss = _dot((x * x).astype(BF16), ones_bd)
    return lax.rsqrt(ss * (1.0 / group) + EPS)


def _rope(x, c, s_up, s_dn):
    w = x.shape[1]
    return x * c + pltpu.roll(x, w - ROT_DIM // 2, 1) * s_up + pltpu.roll(x, ROT_DIM // 2, 1) * s_dn


def _mem_kv_kernel(mem_ref, g_ref, w_ref, gk_ref, ones_ref, k_ref, v_ref):
    x = mem_ref[0]
    y = x * lax.rsqrt(jnp.mean(x * x, axis=-1, keepdims=True) + EPS) * g_ref[...]
    kv = _dot(y.astype(BF16), w_ref[...])
    k = kv[:, :MX_WIDTH]
    k_ref[0] = k * _group_rms_scale(k, ones_ref[...], MX_HEAD_DIM) * gk_ref[...]
    v_ref[0] = kv[:, MX_WIDTH:]


def _mem_kv(mem, mem_norm_g, w_mem_kv, gk_tiled, ones256):
    b, n, _ = mem.shape
    full = lambda shape: pl.BlockSpec(shape, lambda i: (0,) * len(shape))
    return pl.pallas_call(
        _mem_kv_kernel,
        grid=(b,),
        in_specs=[
            pl.BlockSpec((1, n, D_MODEL), lambda i: (i, 0, 0)),
            full((1, D_MODEL)),
            full((D_MODEL, 2 * MX_WIDTH)),
            full((1, MX_WIDTH)),
            full((MX_WIDTH, MX_WIDTH)),
        ],
        out_specs=[pl.BlockSpec((1, n, MX_WIDTH), lambda i: (i, 0, 0))] * 2,
        out_shape=[jax.ShapeDtypeStruct((b, n, MX_WIDTH), F32)] * 2,
        compiler_params=pltpu.CompilerParams(
            dimension_semantics=("arbitrary",), vmem_limit_bytes=VMEM_LIMIT_BYTES),
        name="mem_kv",
    )(mem, mem_norm_g, w_mem_kv, gk_tiled, ones256)


def _lru_scan(a, b):
    tm = a.shape[0]
    row = lax.broadcasted_iota(jnp.int32, a.shape, 0)
    d = 1
    while d < tm:
        keep = row >= d
        b = a * jnp.where(keep, pltpu.roll(b, d, 0), 0.0) + b
        a = a * jnp.where(keep, pltpu.roll(a, d, 0), 1.0)
        d *= 2
    return b


def _in_proj_kernel(emit_attn_layouts, tm,
                    x_ref, cbuf_ref, h0_ref, w_ref, ng_ref, gq_ref, gk_ref, gmq_ref, ones_ref,
                    rc_ref, ru_ref, rd_ref, cw_ref, cb_ref, wa_ref, ba_ref, wx_ref, bx_ref,
                    lam_ref, mk_ref, mv_ref, *rest):
    if emit_attn_layouts:
        (q_ref, k_ref, v_ref, gd_ref, mix_ref, cout_ref, hout_ref, kb_ref, vt_ref,
         xbuf, hc) = rest
    else:
        q_ref, k_ref, v_ref, gd_ref, mix_ref, cout_ref, hout_ref, xbuf, hc = rest
    t = pl.program_id(1)

    @pl.when(t == 0)
    def _():
        xbuf[pl.ds(CONV_PAD - (CONV_W - 1), CONV_W - 1), :] = cbuf_ref[0]
        hc[...] = h0_ref[0]

    x = x_ref[0]
    hn = (x * lax.rsqrt(jnp.mean(x * x, axis=-1, keepdims=True) + EPS) * ng_ref[...]).astype(BF16)

    def proj(off, width):
        return _dot(hn, w_ref[:, off:off + width])

    rc = jnp.concatenate([rc_ref[...]] * (DA_QK // LANES), axis=1)
    ru = jnp.concatenate([ru_ref[...]] * (DA_QK // LANES), axis=1)
    rd = jnp.concatenate([rd_ref[...]] * (DA_QK // LANES), axis=1)
    ones512 = ones_ref[...]

    dq = proj(OFF_DQ, DA_QK)
    q = _rope(dq * _group_rms_scale(dq, ones512, DA_HEAD_DIM) * gq_ref[...], rc, ru, rd)
    q_ref[0] = (q * (DA_HEAD_DIM ** -0.5)).astype(BF16)
    dk = proj(OFF_DK, DA_QK)
    k = _rope(dk * _group_rms_scale(dk, ones512, DA_HEAD_DIM) * gk_ref[...], rc, ru, rd)
    k_ref[0] = k
    dv = proj(OFF_DV, DA_WIDTH)
    v_ref[0] = dv
    if emit_attn_layouts:
        kb_ref[0] = k.astype(BF16)
        for h in range(DA_HEADS):
            vt_ref[0, h, 0] = dv[:, h * DA_VDIM:(h + 1) * DA_VDIM].T.astype(BF16)
    gd_ref[0] = _silu(proj(OFF_DG, DA_WIDTH)).astype(BF16)

    lx = proj(OFF_LX, LRU_WIDTH)
    xbuf[pl.ds(CONV_PAD, tm), :] = lx
    xc = cb_ref[...]
    for j in range(CONV_W):
        xc = xc + xbuf[pl.ds(CONV_PAD - (CONV_W - 1) + j, tm), :] * cw_ref[j:j + 1, :]
    tail = xbuf[pl.ds(CONV_PAD + tm - (CONV_W - 1), CONV_W - 1), :]
    cout_ref[0] = tail
    xbuf[pl.ds(CONV_PAD - (CONV_W - 1), CONV_W - 1), :] = tail
    xcb = xc.astype(BF16)
    r = _sigmoid(_dot(xcb, wa_ref[...]) + ba_ref[...])
    i = _sigmoid(_dot(xcb, wx_ref[...]) + bx_ref[...])
    neg_lam = -lam_ref[...]
    softplus = jnp.maximum(neg_lam, 0.0) + jnp.log1p(jnp.exp(-jnp.abs(neg_lam)))
    log_a = (-LRU_C) * r * softplus
    a = jnp.exp(log_a)
    one_m_a = jnp.where(a == 1.0, -log_a, jnp.where(a == 0.0, 1.0, (1.0 - a) * log_a / jnp.log(a)))
    bb = jnp.sqrt(one_m_a * (1.0 + a)) * (i * xc)
    row = lax.broadcasted_iota(jnp.int32, a.shape, 0)
    bb = bb + jnp.where(row == 0, a * hc[...], 0.0)
    hseq = _lru_scan(a, bb)
    h_last = hseq[tm - 1:tm, :]
    hc[...] = h_last
    hout_ref[0] = h_last
    out_b = hseq * _silu(proj(OFF_LG, LRU_WIDTH))

    mq = proj(OFF_MQ, MX_WIDTH)
    qm = (mq * _group_rms_scale(mq, ones512[:MX_WIDTH, :MX_WIDTH], MX_HEAD_DIM) * gmq_ref[...]
          * (MX_HEAD_DIM ** -0.5)).astype(BF16)
    mk = mk_ref[0].astype(BF16)
    mv = mv_ref[0].astype(BF16)
    lane_q = lax.broadcasted_iota(jnp.int32, (tm, LANES), 1)
    lane_v = lax.broadcasted_iota(jnp.int32, (N_MEM, LANES), 1)
    slabs = []
    for sl in range(MX_WIDTH // LANES):
        qs = qm[:, sl * LANES:(sl + 1) * LANES]
        ks = mk[:, sl * LANES:(sl + 1) * LANES]
        vs = mv[:, sl * LANES:(sl + 1) * LANES]
        acc = None
        rl = None
        for half in range(2):
            sel_q = (lane_q >= MX_HEAD_DIM) if half else (lane_q < MX_HEAD_DIM)
            sel_v = (lane_v >= MX_HEAD_DIM) if half else (lane_v < MX_HEAD_DIM)
            s = _dot_nt(jnp.where(sel_q, qs, jnp.zeros_like(qs)), ks)
            p = jnp.exp(s - jnp.max(s, axis=-1, keepdims=True))
            rsum = 1.0 / jnp.sum(p, axis=-1, keepdims=True)
            o = _dot(p.astype(BF16), jnp.where(sel_v, vs, jnp.zeros_like(vs)))
            acc = o if acc is None else acc + o
            rl = rsum if rl is None else jnp.where(sel_q, rsum, rl)
        slabs.append(acc * rl)
    om = jnp.concatenate(slabs, axis=1)
    out_c = om * _silu(proj(OFF_MG, MX_WIDTH))
    mix_ref[0] = jnp.concatenate([out_b, out_c], axis=1).astype(BF16)


def _in_proj(x, cbuf, h0, mk, mv, wts, rope_tabs, tm, emit_attn_layouts):
    b, t, _ = x.shape
    nt = t // tm
    full = lambda shape: pl.BlockSpec(shape, lambda i, j: (0,) * len(shape))
    per_b = lambda shape: pl.BlockSpec(shape, lambda i, j: (i,) + (0,) * (len(shape) - 1))
    tile = lambda width: pl.BlockSpec((1, tm, width), lambda i, j: (i, j, 0))
    rope_spec = pl.BlockSpec((tm, LANES), lambda i, j: (j, 0))
    in_specs = [
        tile(D_MODEL), per_b((1, CONV_W - 1, LRU_WIDTH)), per_b((1, 1, LRU_WIDTH)),
        full((D_MODEL, IN_WIDTH)), full((1, D_MODEL)), full((1, DA_QK)), full((1, DA_QK)),
        full((1, MX_WIDTH)), full((DA_QK, DA_QK)),
        rope_spec, rope_spec, rope_spec,
        full((CONV_W, LRU_WIDTH)), full((1, LRU_WIDTH)),
        full((LRU_WIDTH, LRU_WIDTH)), full((1, LRU_WIDTH)),
        full((LRU_WIDTH, LRU_WIDTH)), full((1, LRU_WIDTH)), full((1, LRU_WIDTH)),
        per_b((1, N_MEM, MX_WIDTH)), per_b((1, N_MEM, MX_WIDTH)),
    ]
    out_specs = [tile(DA_QK), tile(DA_QK), tile(DA_WIDTH), tile(DA_WIDTH), tile(DA_WIDTH),
                 per_b((1, CONV_W - 1, LRU_WIDTH)), per_b((1, 1, LRU_WIDTH))]
    out_shape = [jax.ShapeDtypeStruct((b, t, DA_QK), BF16),
                 jax.ShapeDtypeStruct((b, t, DA_QK), F32),
                 jax.ShapeDtypeStruct((b, t, DA_WIDTH), F32),
                 jax.ShapeDtypeStruct((b, t, DA_WIDTH), BF16),
                 jax.ShapeDtypeStruct((b, t, DA_WIDTH), BF16),
                 jax.ShapeDtypeStruct((b, CONV_W - 1, LRU_WIDTH), F32),
                 jax.ShapeDtypeStruct((b, 1, LRU_WIDTH), F32)]
    if emit_attn_layouts:
        out_specs += [tile(DA_QK),
                      pl.BlockSpec((1, DA_HEADS, 1, DA_VDIM, tm), lambda i, j: (i, 0, j, 0, 0))]
        out_shape += [jax.ShapeDtypeStruct((b, t, DA_QK), BF16),
                      jax.ShapeDtypeStruct((b, DA_HEADS, nt, DA_VDIM, tm), BF16)]
    return pl.pallas_call(
        functools.partial(_in_proj_kernel, emit_attn_layouts, tm),
        grid=(b, nt),
        in_specs=in_specs,
        out_specs=out_specs,
        out_shape=out_shape,
        scratch_shapes=[pltpu.VMEM((CONV_PAD + tm, LRU_WIDTH), F32), pltpu.VMEM((1, LRU_WIDTH), F32)],
        compiler_params=pltpu.CompilerParams(
            dimension_semantics=("arbitrary", "arbitrary"), vmem_limit_bytes=VMEM_LIMIT_BYTES),
        name="in_proj_prompt" if emit_attn_layouts else "in_proj_sample",
    )(x, cbuf, h0, wts["w_in"], wts["norm_g"], wts["gq"], wts["gk"], wts["gmq"], wts["ones512"],
      *rope_tabs, wts["conv_w"], wts["conv_b"], wts["wa"], wts["ba"], wts["wx"], wts["bx"],
      wts["lru_lambda"], mk, mv)


def _lambda_value(lam_ref):
    lp = lam_ref[...]
    s1 = jnp.sum(lp[0:1] * lp[1:2], axis=-1, keepdims=True)
    s2 = jnp.sum(lp[2:3] * lp[3:4], axis=-1, keepdims=True)
    return jnp.exp(s1) - jnp.exp(s2) + LAMBDA_INIT


def _attn_prompt_kernel(tq, tk, q_ref, k_ref, vt_ref, gd_ref, sg_ref, lam_ref, o_ref,
                        m_scr, l_scr, acc_scr):
    qi = pl.program_id(2)
    q = q_ref[0]
    lane = lax.broadcasted_iota(jnp.int32, q.shape, 1)
    zero = jnp.zeros_like(q)
    w = jnp.concatenate([jnp.where(lane < DA_HEAD_DIM, q, zero),
                         jnp.where(lane >= DA_HEAD_DIM, q, zero)], axis=0)
    m_scr[...] = jnp.full(m_scr.shape, NEG_BIG, F32)
    l_scr[...] = jnp.zeros(l_scr.shape, F32)
    acc_scr[...] = jnp.zeros(acc_scr.shape, F32)

    def step(j, masked):
        k_t = k_ref[0, pl.ds(pl.multiple_of(j * tk, tk), tk), :]
        s = _dot_nt(k_t, w)
        if masked:
            kpos = j * tk + lax.broadcasted_iota(jnp.int32, s.shape, 0)
            col = lax.broadcasted_iota(jnp.int32, s.shape, 1)
            qpos = qi * tq + jnp.where(col >= tq, col - tq, col)
            s = jnp.where((kpos >> 6) <= (qpos >> 6), s, NEG_BIG)
        m_old = m_scr[...]
        m_new = jnp.maximum(m_old, jnp.max(s, axis=0, keepdims=True))
        alpha = jnp.exp(m_old - m_new)
        p = jnp.exp(s - m_new)
        l_scr[...] = alpha * l_scr[...] + jnp.sum(p, axis=0, keepdims=True)
        acc_scr[...] = alpha * acc_scr[...] + _dot(vt_ref[0, 0, j], p.astype(BF16))
        m_scr[...] = m_new

    n_full = (qi * tq) // tk
    n_total = ((qi + 1) * tq + tk - 1) // tk

    def full_body(j, c):
        step(j, False)
        return c

    def diag_body(j, c):
        step(j, True)
        return c

    lax.fori_loop(0, n_full, full_body, 0)
    lax.fori_loop(n_full, n_total, diag_body, 0)

    inv_l = 1.0 / l_scr[...]
    acc = acc_scr[...] * inv_l
    ot = acc[:, :tq] - _lambda_value(lam_ref) * acc[:, tq:]
    ot = ot * lax.rsqrt(jnp.mean(ot * ot, axis=0, keepdims=True) + EPS)
    o = ot.T * (sg_ref[...] * (1.0 - LAMBDA_INIT))
    o_ref[0] = (o * gd_ref[0].astype(F32)).astype(BF16)


def _attn_prompt(q, kb, vt, gd, subln_g, lam_params, tq, tk):
    b, t, _ = q.shape
    nk = t // tk
    return pl.pallas_call(
        functools.partial(_attn_prompt_kernel, tq, tk),
        grid=(b, DA_HEADS, t // tq),
        in_specs=[
            pl.BlockSpec((1, tq, LANES), lambda i, h, j: (i, j, h)),
            pl.BlockSpec((1, t, LANES), lambda i, h, j: (i, 0, h)),
            pl.BlockSpec((1, 1, nk, DA_VDIM, tk), lambda i, h, j: (i, h, 0, 0, 0)),
            pl.BlockSpec((1, tq, LANES), lambda i, h, j: (i, j, h)),
            pl.BlockSpec((1, DA_VDIM), lambda i, h, j: (0, 0)),
            pl.BlockSpec((4, DA_HEAD_DIM), lambda i, h, j: (0, 0)),
        ],
        out_specs=pl.BlockSpec((1, tq, LANES), lambda i, h, j: (i, j, h)),
        out_shape=jax.ShapeDtypeStruct((b, t, DA_WIDTH), BF16),
        scratch_shapes=[pltpu.VMEM((1, 2 * tq), F32), pltpu.VMEM((1, 2 * tq), F32),
                        pltpu.VMEM((DA_VDIM, 2 * tq), F32)],
        compiler_params=pltpu.CompilerParams(
            dimension_semantics=("arbitrary", "arbitrary", "arbitrary"),
            vmem_limit_bytes=VMEM_LIMIT_BYTES),
        name="diff_attn_prompt",
    )(q, kb, vt, gd, subln_g, lam_params)


def _attn_sample_kernel(past, q_ref, kn_ref, vn_ref, kp_ref, vp_ref, gd_ref, sg_ref, lam_ref, o_ref):
    ts = q_ref.shape[1]
    lam = _lambda_value(lam_ref)
    lane = lax.broadcasted_iota(jnp.int32, (ts, LANES), 1)
    qrow = past + lax.broadcasted_iota(jnp.int32, (ts, past), 0)
    vis_p = (lax.broadcasted_iota(jnp.int32, (ts, past), 1) >> 6) <= (qrow >> 6)
    qrow_n = past + lax.broadcasted_iota(jnp.int32, (ts, ts), 0)
    vis_n = ((past + lax.broadcasted_iota(jnp.int32, (ts, ts), 1)) >> 6) <= (qrow_n >> 6)
    for h in range(DA_HEADS):
        sl = slice(h * LANES, (h + 1) * LANES)
        q = q_ref[0, :, sl]
        kp = kp_ref[0, :, sl].astype(BF16)
        kn = kn_ref[0, :, sl].astype(BF16)
        zero = jnp.zeros_like(q)
        wp, wn = [], []
        for c in range(2):
            qc = jnp.where((lane >= DA_HEAD_DIM) if c else (lane < DA_HEAD_DIM), q, zero)
            sp = jnp.where(vis_p, _dot_nt(qc, kp), NEG_BIG)
            sn = jnp.where(vis_n, _dot_nt(qc, kn), NEG_BIG)
            m = jnp.maximum(jnp.max(sp, axis=-1, keepdims=True), jnp.max(sn, axis=-1, keepdims=True))
            pp = jnp.exp(sp - m)
            pn = jnp.exp(sn - m)
            inv = 1.0 / (jnp.sum(pp, axis=-1, keepdims=True) + jnp.sum(pn, axis=-1, keepdims=True))
            wp.append(pp * inv)
            wn.append(pn * inv)
        wgt_p = (wp[0] - lam * wp[1]).astype(BF16)
        wgt_n = (wn[0] - lam * wn[1]).astype(BF16)
        o = _dot(wgt_p, vp_ref[0, :, sl].astype(BF16)) + _dot(wgt_n, vn_ref[0, :, sl].astype(BF16))
        o = o * lax.rsqrt(jnp.mean(o * o, axis=-1, keepdims=True) + EPS)
        o = o * (sg_ref[...] * (1.0 - LAMBDA_INIT))
        o_ref[0, :, sl] = (o * gd_ref[0, :, sl].astype(F32)).astype(BF16)


def _attn_sample(q, k_new, v_new, k_past, v_past, gd, subln_g, lam_params):
    b, ts, _ = q.shape
    past = k_past.shape[1]
    new = pl.BlockSpec((1, ts, DA_WIDTH), lambda i: (i, 0, 0))
    old = pl.BlockSpec((1, past, DA_WIDTH), lambda i: (i, 0, 0))
    return pl.pallas_call(
        functools.partial(_attn_sample_kernel, past),
        grid=(b,),
        in_specs=[new, new, new, old, old, new,
                  pl.BlockSpec((1, DA_VDIM), lambda i: (0, 0)),
                  pl.BlockSpec((4, DA_HEAD_DIM), lambda i: (0, 0))],
        out_specs=new,
        out_shape=jax.ShapeDtypeStruct((b, ts, DA_WIDTH), BF16),
        compiler_params=pltpu.CompilerParams(
            dimension_semantics=("arbitrary",), vmem_limit_bytes=VMEM_LIMIT_BYTES),
        name="diff_attn_sample",
    )(q, k_new, v_new, k_past, v_past, gd, subln_g, lam_params)


def _out_proj_kernel(x_ref, a_ref, mix_ref, w_ref, y_ref):
    y_ref[0] = (x_ref[0] + _dot(a_ref[0], w_ref[:DA_WIDTH, :]) + _dot(mix_ref[0], w_ref[DA_WIDTH:, :]))


def _out_proj(x, out_a, mix, w_out, tm):
    b, t, _ = x.shape
    tile = lambda width: pl.BlockSpec((1, tm, width), lambda i, j: (i, j, 0))
    return pl.pallas_call(
        _out_proj_kernel,
        grid=(b, t // tm),
        in_specs=[tile(D_MODEL), tile(DA_WIDTH), tile(DA_WIDTH),
                  pl.BlockSpec((D_MODEL, D_MODEL), lambda i, j: (0, 0))],
        out_specs=tile(D_MODEL),
        out_shape=jax.ShapeDtypeStruct((b, t, D_MODEL), F32),
        compiler_params=pltpu.CompilerParams(
            dimension_semantics=("arbitrary", "arbitrary"), vmem_limit_bytes=VMEM_LIMIT_BYTES),
        name="out_proj",
    )(x, out_a, mix, w_out)


def _rope_tables(pos):
    half = ROT_DIM // 2
    inv = ROPE_THETA ** (-jnp.arange(0, ROT_DIM, 2, dtype=F32) / ROT_DIM)
    ang = pos.astype(F32)[:, None] * inv[None, :]
    cos, sin = jnp.cos(ang), jnp.sin(ang)
    t = pos.shape[0]
    ones = jnp.ones((t, DA_HEAD_DIM - ROT_DIM), F32)
    zeros = jnp.zeros((t, DA_HEAD_DIM - ROT_DIM), F32)
    zh = jnp.zeros((t, half), F32)
    c = jnp.concatenate([cos, cos, ones], axis=1)
    s_up = jnp.concatenate([-sin, zh, zeros], axis=1)
    s_dn = jnp.concatenate([zh, sin, zeros], axis=1)
    rep = LANES // DA_HEAD_DIM
    return tuple(jnp.tile(a, (1, rep)) for a in (c, s_up, s_dn))


def _block_diag(w):
    n, bi, bj = w.shape
    eye = jnp.eye(n, dtype=w.dtype)
    return (eye[:, None, :, None] * w[:, :, None, :]).reshape(n * bi, n * bj)


def _pick_tile(t, pref):
    return pref if t % pref == 0 else t


def kernel(x_prompt, x_sample, mem_prompt, cache_diff_k, cache_diff_v, cache_mem_k, cache_mem_v,
           state_lru_conv, state_lru_h, norm_g, w_in, da_q_norm_g, da_k_norm_g, lambda_q1, lambda_k1,
           lambda_q2, lambda_k2, da_subln_g, lru_conv_w, lru_conv_b, lru_w_a, lru_b_a, lru_w_x, lru_b_x,
           lru_lambda, mem_norm_g, w_mem_kv, mx_q_norm_g, mx_k_norm_g, w_out):
    depth = w_in.shape[0]
    assert depth == 1
    bp, tp, _ = x_prompt.shape
    bs, ts, _ = x_sample.shape
    past = cache_diff_k.shape[2]
    l = 0
    grp = np.arange(DA_QK) // DA_HEAD_DIM
    ones512 = jnp.asarray(grp[:, None] == grp[None, :], BF16)
    wts = dict(
        w_in=w_in[l].astype(BF16), norm_g=norm_g[l][None, :],
        gq=jnp.tile(da_q_norm_g[l], DA_QK // DA_HEAD_DIM)[None, :],
        gk=jnp.tile(da_k_norm_g[l], DA_QK // DA_HEAD_DIM)[None, :],
        gmq=jnp.tile(mx_q_norm_g[l], MX_HEADS)[None, :], ones512=ones512,
        conv_w=lru_conv_w[l], conv_b=lru_conv_b[l][None, :],
        wa=_block_diag(lru_w_a[l]).astype(BF16), ba=lru_b_a[l][None, :],
        wx=_block_diag(lru_w_x[l]).astype(BF16), bx=lru_b_x[l][None, :],
        lru_lambda=lru_lambda[l][None, :])
    lam_params = jnp.stack([lambda_q1[l], lambda_k1[l], lambda_q2[l], lambda_k2[l]])
    subln_g = da_subln_g[l][None, :]
    w_out_b = w_out[l].astype(BF16)

    mk_p, mv_p = _mem_kv(mem_prompt, mem_norm_g[l][None, :], w_mem_kv[l].astype(BF16),
                         jnp.tile(mx_k_norm_g[l], MX_HEADS)[None, :], ones512[:MX_WIDTH, :MX_WIDTH])
    tm_p = _pick_tile(tp, 512)
    tq_p = _pick_tile(tp, 256)
    pos_p = jnp.arange(tp, dtype=jnp.int32)
    q_p, k_p, v_p, gd_p, mix_p, c_p, h_p, kb_p, vt_p = _in_proj(
        x_prompt, jnp.zeros((bp, CONV_W - 1, LRU_WIDTH), F32), jnp.zeros((bp, 1, LRU_WIDTH), F32),
        mk_p, mv_p, wts, _rope_tables(pos_p), tm_p, True)
    oa_p = _attn_prompt(q_p, kb_p, vt_p, gd_p, subln_g, lam_params, tq_p, tm_p)
    y_p = _out_proj(x_prompt, oa_p, mix_p, w_out_b, tm_p)

    pos_s = past + jnp.arange(ts, dtype=jnp.int32)
    q_s, k_s, v_s, gd_s, mix_s, c_s, h_s = _in_proj(
        x_sample, state_lru_conv[l], state_lru_h[l][:, None, :],
        cache_mem_k[l].reshape(bs, N_MEM, MX_WIDTH), cache_mem_v[l].reshape(bs, N_MEM, MX_WIDTH),
        wts, _rope_tables(pos_s), ts, False)
    oa_s = _attn_sample(q_s, k_s, v_s, cache_diff_k[l].reshape(bs, past, DA_QK),
                        cache_diff_v[l].reshape(bs, past, DA_WIDTH), gd_s, subln_g, lam_params)
    y_s = _out_proj(x_sample, oa_s, mix_s, w_out_b, ts)

    return (y_p, y_s,
            k_p.reshape(1, bp, tp, DA_HEADS, 2, DA_HEAD_DIM), v_p.reshape(1, bp, tp, DA_HEADS, DA_VDIM),
            mk_p.reshape(1, bp, N_MEM, MX_HEADS, MX_HEAD_DIM), mv_p.reshape(1, bp, N_MEM, MX_HEADS, MX_HEAD_DIM),
            c_p[None], h_p.reshape(1, bp, LRU_WIDTH),
            k_s.reshape(1, bs, ts, DA_HEADS, 2, DA_HEAD_DIM), v_s.reshape(1, bs, ts, DA_HEADS, DA_VDIM),
            c_s[None], h_s.reshape(1, bs, LRU_WIDTH))
```

```python
import functools
import math

import numpy as np
import jax
import jax.numpy as jnp
from jax import lax
from jax.experimental import pallas as pl
from jax.experimental.pallas import tpu as pltpu

D_MODEL = 1024
CHUNK = 64
EPS = 1e-6
DA_HEADS = 4
DA_HEAD_DIM = 64
DA_VDIM = 128
DA_QK = 512
DA_WIDTH = 512
ROPE_THETA = 500000.0
ROT_DIM = 16
LRU_WIDTH = 256
LRU_BLOCKS = 4
CONV_W = 4
LRU_C = 8.0
MX_HEADS = 4
MX_HEAD_DIM = 64
MX_WIDTH = 256
N_MEM = 256
LAMBDA_INIT = 0.8 - 0.6 * math.exp(-0.3 * 0)
NEG_BIG = -1e30
LOG2E = math.log2(math.e)

OFF_DQ, OFF_DK, OFF_DV, OFF_DG = 0, 512, 1024, 1536
OFF_LX, OFF_LG, OFF_MQ, OFF_MG = 2048, 2304, 2560, 2816
IN_WIDTH = 3072

VMEM_LIMIT_BYTES = 52 * 1024 * 1024
LANES = 128
CONV_PAD = 8

F32 = jnp.float32
BF16 = jnp.bfloat16


def _dot(a, b):
    return jnp.dot(a, b, preferred_element_type=F32)


def _dot_nt(a, b):
    return lax.dot_general(a, b, (((1,), (1,)), ((), ())), preferred_element_type=F32)


def _sigmoid(x):
    return 1.0 / (1.0 + jnp.exp(-x))


def _silu(x):
    return x * _sigmoid(x)


def _group_rms_scale(x, ones_bd, group):
    ss = _dot((x * x).astype(BF16), ones_bd)
    return lax.rsqrt(ss * (1.0 / group) + EPS)


def _rope(x, c, s_up, s_dn):
    w = x.shape[1]
    return x * c + pltpu.roll(x, w - ROT_DIM // 2, 1) * s_up + pltpu.roll(x, ROT_DIM // 2, 1) * s_dn


def _mem_kv_kernel(mem_ref, g_ref, w_ref, gk_ref, ones_ref, k_ref, v_ref):
    x = mem_ref[0]
    y = x * lax.rsqrt(jnp.mean(x * x, axis=-1, keepdims=True) + EPS) * g_ref[...]
    kv = _dot(y.astype(BF16), w_ref[...])
    k = kv[:, :MX_WIDTH]
    k_ref[0] = k * _group_rms_scale(k, ones_ref[...], MX_HEAD_DIM) * gk_ref[...]
    v_ref[0] = kv[:, MX_WIDTH:]


def _mem_kv(mem, mem_norm_g, w_mem_kv, gk_tiled, ones256):
    b, n, _ = mem.shape
    full = lambda shape: pl.BlockSpec(shape, lambda i: (0,) * len(shape))
    return pl.pallas_call(
        _mem_kv_kernel,
        grid=(b,),
        in_specs=[
            pl.BlockSpec((1, n, D_MODEL), lambda i: (i, 0, 0)),
            full((1, D_MODEL)),
            full((D_MODEL, 2 * MX_WIDTH)),
            full((1, MX_WIDTH)),
            full((MX_WIDTH, MX_WIDTH)),
        ],
        out_specs=[pl.BlockSpec((1, n, MX_WIDTH), lambda i: (i, 0, 0))] * 2,
        out_shape=[jax.ShapeDtypeStruct((b, n, MX_WIDTH), F32)] * 2,
        compiler_params=pltpu.CompilerParams(
            dimension_semantics=("arbitrary",), vmem_limit_bytes=VMEM_LIMIT_BYTES),
        name="mem_kv",
    )(mem, mem_norm_g, w_mem_kv, gk_tiled, ones256)


def _lru_scan(a, b):
    tm = a.shape[0]
    row = lax.broadcasted_iota(jnp.int32, a.shape, 0)
    d = 1
    while d < tm:
        keep = row >= d
        b = a * jnp.where(keep, pltpu.roll(b, d, 0), 0.0) + b
        a = a * jnp.where(keep, pltpu.roll(a, d, 0), 1.0)
        d *= 2
    return b


def _in_proj_kernel(emit_attn_layouts, tm,
                    x_ref, cbuf_ref, h0_ref, w_ref, ng_ref, gq_ref, gk_ref, gmq_ref, ones_ref,
                    rc_ref, ru_ref, rd_ref, cw_ref, cb_ref, wa_ref, ba_ref, wx_ref, bx_ref,
                    lam_ref, mk_ref, mv_ref, *rest):
    if emit_attn_layouts:
        (q_ref, k_ref, v_ref, gd_ref, mix_ref, cout_ref, hout_ref, kb_ref, vt_ref,
         xbuf, hc) = rest
    else:
        q_ref, k_ref, v_ref, gd_ref, mix_ref, cout_ref, hout_ref, xbuf, hc = rest
    t = pl.program_id(1)

    @pl.when(t == 0)
    def _():
        xbuf[pl.ds(CONV_PAD - (CONV_W - 1), CONV_W - 1), :] = cbuf_ref[0]
        hc[...] = h0_ref[0]

    x = x_ref[0]
    hn = (x * lax.rsqrt(jnp.mean(x * x, axis=-1, keepdims=True) + EPS) * ng_ref[...]).astype(BF16)

    def proj(off, width):
        return _dot(hn, w_ref[:, off:off + width])

    rc = jnp.concatenate([rc_ref[...]] * (DA_QK // LANES), axis=1)
    ru = jnp.concatenate([ru_ref[...]] * (DA_QK // LANES), axis=1)
    rd = jnp.concatenate([rd_ref[...]] * (DA_QK // LANES), axis=1)
    ones512 = ones_ref[...]

    dq = proj(OFF_DQ, DA_QK)
    q = _rope(dq * _group_rms_scale(dq, ones512, DA_HEAD_DIM) * gq_ref[...], rc, ru, rd)
    q_ref[0] = (q * (DA_HEAD_DIM ** -0.5 * LOG2E)).astype(BF16)
    dk = proj(OFF_DK, DA_QK)
    k = _rope(dk * _group_rms_scale(dk, ones512, DA_HEAD_DIM) * gk_ref[...], rc, ru, rd)
    k_ref[0] = k
    dv = proj(OFF_DV, DA_WIDTH)
    v_ref[0] = dv
    if emit_attn_layouts:
        kb_ref[0] = k.astype(BF16)
        for h in range(DA_HEADS):
            vt_ref[0, h, 0] = dv[:, h * DA_VDIM:(h + 1) * DA_VDIM].T.astype(BF16)
    gd_ref[0] = _silu(proj(OFF_DG, DA_WIDTH)).astype(BF16)

    lx = proj(OFF_LX, LRU_WIDTH)
    xbuf[pl.ds(CONV_PAD, tm), :] = lx
    xc = cb_ref[...]
    for j in range(CONV_W):
        xc = xc + xbuf[pl.ds(CONV_PAD - (CONV_W - 1) + j, tm), :] * cw_ref[j:j + 1, :]
    tail = xbuf[pl.ds(CONV_PAD + tm - (CONV_W - 1), CONV_W - 1), :]
    cout_ref[0] = tail
    xbuf[pl.ds(CONV_PAD - (CONV_W - 1), CONV_W - 1), :] = tail
    xcb = xc.astype(BF16)
    r = _sigmoid(_dot(xcb, wa_ref[...]) + ba_ref[...])
    i = _sigmoid(_dot(xcb, wx_ref[...]) + bx_ref[...])
    neg_lam = -lam_ref[...]
    softplus = jnp.maximum(neg_lam, 0.0) + jnp.log1p(jnp.exp(-jnp.abs(neg_lam)))
    log_a = (-LRU_C) * r * softplus
    a = jnp.exp(log_a)
    one_m_a = jnp.where(a == 1.0, -log_a, jnp.where(a == 0.0, 1.0, (1.0 - a) * log_a / jnp.log(a)))
    bb = jnp.sqrt(one_m_a * (1.0 + a)) * (i * xc)
    row = lax.broadcasted_iota(jnp.int32, a.shape, 0)
    bb = bb + jnp.where(row == 0, a * hc[...], 0.0)
    hseq = _lru_scan(a, bb)
    h_last = hseq[tm - 1:tm, :]
    hc[...] = h_last
    hout_ref[0] = h_last
    out_b = hseq * _silu(proj(OFF_LG, LRU_WIDTH))

    mq = proj(OFF_MQ, MX_WIDTH)
    qm = (mq * _group_rms_scale(mq, ones512[:MX_WIDTH, :MX_WIDTH], MX_HEAD_DIM) * gmq_ref[...]
          * (MX_HEAD_DIM ** -0.5)).astype(BF16)
    mk = mk_ref[0].astype(BF16)
    mv = mv_ref[0].astype(BF16)
    lane_q = lax.broadcasted_iota(jnp.int32, (tm, LANES), 1)
    lane_v = lax.broadcasted_iota(jnp.int32, (N_MEM, LANES), 1)
    slabs = []
    for sl in range(MX_WIDTH // LANES):
        qs = qm[:, sl * LANES:(sl + 1) * LANES]
        ks = mk[:, sl * LANES:(sl + 1) * LANES]
        vs = mv[:, sl * LANES:(sl + 1) * LANES]
        acc = None
        rl = None
        for half in range(2):
            sel_q = (lane_q >= MX_HEAD_DIM) if half else (lane_q < MX_HEAD_DIM)
            sel_v = (lane_v >= MX_HEAD_DIM) if half else (lane_v < MX_HEAD_DIM)
            s = _dot_nt(jnp.where(sel_q, qs, jnp.zeros_like(qs)), ks)
            p = jnp.exp(s - jnp.max(s, axis=-1, keepdims=True))
            rsum = 1.0 / jnp.sum(p, axis=-1, keepdims=True)
            o = _dot(p.astype(BF16), jnp.where(sel_v, vs, jnp.zeros_like(vs)))
            acc = o if acc is None else acc + o
            rl = rsum if rl is None else jnp.where(sel_q, rsum, rl)
        slabs.append(acc * rl)
    om = jnp.concatenate(slabs, axis=1)
    out_c = om * _silu(proj(OFF_MG, MX_WIDTH))
    mix_ref[0] = jnp.concatenate([out_b, out_c], axis=1).astype(BF16)


def _in_proj(x, cbuf, h0, mk, mv, wts, rope_tabs, tm, emit_attn_layouts):
    b, t, _ = x.shape
    nt = t // tm
    full = lambda shape: pl.BlockSpec(shape, lambda i, j: (0,) * len(shape))
    per_b = lambda shape: pl.BlockSpec(shape, lambda i, j: (i,) + (0,) * (len(shape) - 1))
    tile = lambda width: pl.BlockSpec((1, tm, width), lambda i, j: (i, j, 0))
    rope_spec = pl.BlockSpec((tm, LANES), lambda i, j: (j, 0))
    in_specs = [
        tile(D_MODEL), per_b((1, CONV_W - 1, LRU_WIDTH)), per_b((1, 1, LRU_WIDTH)),
        full((D_MODEL, IN_WIDTH)), full((1, D_MODEL)), full((1, DA_QK)), full((1, DA_QK)),
        full((1, MX_WIDTH)), full((DA_QK, DA_QK)),
        rope_spec, rope_spec, rope_spec,
        full((CONV_W, LRU_WIDTH)), full((1, LRU_WIDTH)),
        full((LRU_WIDTH, LRU_WIDTH)), full((1, LRU_WIDTH)),
        full((LRU_WIDTH, LRU_WIDTH)), full((1, LRU_WIDTH)), full((1, LRU_WIDTH)),
        per_b((1, N_MEM, MX_WIDTH)), per_b((1, N_MEM, MX_WIDTH)),
    ]
    out_specs = [tile(DA_QK), tile(DA_QK), tile(DA_WIDTH), tile(DA_WIDTH), tile(DA_WIDTH),
                 per_b((1, CONV_W - 1, LRU_WIDTH)), per_b((1, 1, LRU_WIDTH))]
    out_shape = [jax.ShapeDtypeStruct((b, t, DA_QK), BF16),
                 jax.ShapeDtypeStruct((b, t, DA_QK), F32),
                 jax.ShapeDtypeStruct((b, t, DA_WIDTH), F32),
                 jax.ShapeDtypeStruct((b, t, DA_WIDTH), BF16),
                 jax.ShapeDtypeStruct((b, t, DA_WIDTH), BF16),
                 jax.ShapeDtypeStruct((b, CONV_W - 1, LRU_WIDTH), F32),
                 jax.ShapeDtypeStruct((b, 1, LRU_WIDTH), F32)]
    if emit_attn_layouts:
        out_specs += [tile(DA_QK),
                      pl.BlockSpec((1, DA_HEADS, 1, DA_VDIM, tm), lambda i, j: (i, 0, j, 0, 0))]
        out_shape += [jax.ShapeDtypeStruct((b, t, DA_QK), BF16),
                      jax.ShapeDtypeStruct((b, DA_HEADS, nt, DA_VDIM, tm), BF16)]
    return pl.pallas_call(
        functools.partial(_in_proj_kernel, emit_attn_layouts, tm),
        grid=(b, nt),
        in_specs=in_specs,
        out_specs=out_specs,
        out_shape=out_shape,
        scratch_shapes=[pltpu.VMEM((CONV_PAD + tm, LRU_WIDTH), F32), pltpu.VMEM((1, LRU_WIDTH), F32)],
        compiler_params=pltpu.CompilerParams(
            dimension_semantics=("arbitrary", "arbitrary"), vmem_limit_bytes=VMEM_LIMIT_BYTES),
        name="in_proj_prompt" if emit_attn_layouts else "in_proj_sample",
    )(x, cbuf, h0, wts["w_in"], wts["norm_g"], wts["gq"], wts["gk"], wts["gmq"], wts["ones512"],
      *rope_tabs, wts["conv_w"], wts["conv_b"], wts["wa"], wts["ba"], wts["wx"], wts["bx"],
      wts["lru_lambda"], mk, mv)


def _lambda_value(lam_ref):
    lp = lam_ref[...]
    s1 = jnp.sum(lp[0:1] * lp[1:2], axis=-1, keepdims=True)
    s2 = jnp.sum(lp[2:3] * lp[3:4], axis=-1, keepdims=True)
    return jnp.exp(s1) - jnp.exp(s2) + LAMBDA_INIT


def _attn_prompt_kernel(tq, tk, q_ref, k_ref, feat_ref, vt_ref, gd_ref, sg_ref, lam_ref, o_ref,
                        w_scr, s_scr, p_scr, tmax_scr, m_scr, l_scr, acc_scr):
    qi = pl.program_id(2)
    q = q_ref[0]
    lane = lax.broadcasted_iota(jnp.int32, q.shape, 1)
    qchunk = (qi * tq + lax.broadcasted_iota(jnp.int32, q.shape, 0)) >> 6
    qfeat = jnp.where(lane > qchunk, NEG_BIG, 0.0).astype(BF16)
    zero = jnp.zeros_like(q)
    w_scr[...] = jnp.concatenate(
        [jnp.concatenate([jnp.where(lane < DA_HEAD_DIM, q, zero), qfeat], axis=1),
         jnp.concatenate([jnp.where(lane >= DA_HEAD_DIM, q, zero), qfeat], axis=1)], axis=0)
    n = ((qi + 1) * tq + tk - 1) // tk

    def scores(j):
        off = pl.multiple_of(j * tk, tk)
        kx = jnp.concatenate([k_ref[0, pl.ds(off, tk), :], feat_ref[pl.ds(off, tk), :]], axis=1)
        return _dot_nt(kx, w_scr[...])

    def stage_scores(j):
        s = scores(j)
        s_scr[...] = s
        tmax_scr[...] = jnp.max(s, axis=0, keepdims=True)

    stage_scores(0)
    p_scr[...] = jnp.zeros(p_scr.shape, BF16)
    m_scr[...] = jnp.full(m_scr.shape, NEG_BIG, F32)
    l_scr[...] = jnp.zeros(l_scr.shape, F32)
    acc_scr[...] = jnp.zeros(acc_scr.shape, F32)

    def body(j, c):
        pv = _dot(vt_ref[0, 0, jnp.maximum(j - 1, 0)], p_scr[...])
        m_old = m_scr[...]
        m_new = jnp.maximum(m_old, tmax_scr[...])
        alpha = jnp.exp2(m_old - m_new)
        p = jnp.exp2(s_scr[...] - m_new)
        l_scr[...] = alpha * l_scr[...] + jnp.sum(p, axis=0, keepdims=True)
        p_scr[...] = p.astype(BF16)
        acc_scr[...] = alpha * (acc_scr[...] + pv)
        m_scr[...] = m_new
        stage_scores(jnp.minimum(j + 1, n - 1))
        return c

    lax.fori_loop(0, n, body, 0)

    acc = (acc_scr[...] + _dot(vt_ref[0, 0, n - 1], p_scr[...])) * (1.0 / l_scr[...])
    ot = acc[:, :tq] - _lambda_value(lam_ref) * acc[:, tq:]
    ot = ot * lax.rsqrt(jnp.mean(ot * ot, axis=0, keepdims=True) + EPS)
    o = ot.T * (sg_ref[...] * (1.0 - LAMBDA_INIT))
    o_ref[0] = (o * gd_ref[0].astype(F32)).astype(BF16)


def _attn_prompt(q, kb, vt, gd, subln_g, lam_params, tq, tk):
    b, t, _ = q.shape
    nk = t // tk
    assert t % CHUNK == 0 and t // CHUNK <= LANES, "chunk one-hot must fit one 128-lane slab"
    key_chunk = np.arange(t) // CHUNK
    feat = jnp.asarray(key_chunk[:, None] == np.arange(LANES)[None, :], BF16)
    return pl.pallas_call(
        functools.partial(_attn_prompt_kernel, tq, tk),
        grid=(b, DA_HEADS, t // tq),
        in_specs=[
            pl.BlockSpec((1, tq, LANES), lambda i, h, j: (i, j, h)),
            pl.BlockSpec((1, t, LANES), lambda i, h, j: (i, 0, h)),
            pl.BlockSpec((t, LANES), lambda i, h, j: (0, 0)),
            pl.BlockSpec((1, 1, nk, DA_VDIM, tk), lambda i, h, j: (i, h, 0, 0, 0)),
            pl.BlockSpec((1, tq, LANES), lambda i, h, j: (i, j, h)),
            pl.BlockSpec((1, DA_VDIM), lambda i, h, j: (0, 0)),
            pl.BlockSpec((4, DA_HEAD_DIM), lambda i, h, j: (0, 0)),
        ],
        out_specs=pl.BlockSpec((1, tq, LANES), lambda i, h, j: (i, j, h)),
        out_shape=jax.ShapeDtypeStruct((b, t, DA_WIDTH), BF16),
        scratch_shapes=[pltpu.VMEM((2 * tq, 2 * LANES), BF16),
                        pltpu.VMEM((tk, 2 * tq), F32), pltpu.VMEM((tk, 2 * tq), BF16),
                        pltpu.VMEM((1, 2 * tq), F32), pltpu.VMEM((1, 2 * tq), F32),
                        pltpu.VMEM((1, 2 * tq), F32), pltpu.VMEM((DA_VDIM, 2 * tq), F32)],
        compiler_params=pltpu.CompilerParams(
            dimension_semantics=("arbitrary", "arbitrary", "arbitrary"),
            vmem_limit_bytes=VMEM_LIMIT_BYTES),
        name="diff_attn_prompt",
    )(q, kb, feat, vt, gd, subln_g, lam_params)


def _attn_sample_kernel(past, q_ref, kn_ref, vn_ref, kp_ref, vp_ref, gd_ref, sg_ref, lam_ref, o_ref):
    ts = q_ref.shape[1]
    lam = _lambda_value(lam_ref)
    lane = lax.broadcasted_iota(jnp.int32, (ts, LANES), 1)
    qrow = past + lax.broadcasted_iota(jnp.int32, (ts, past), 0)
    vis_p = (lax.broadcasted_iota(jnp.int32, (ts, past), 1) >> 6) <= (qrow >> 6)
    qrow_n = past + lax.broadcasted_iota(jnp.int32, (ts, ts), 0)
    vis_n = ((past + lax.broadcasted_iota(jnp.int32, (ts, ts), 1)) >> 6) <= (qrow_n >> 6)
    for h in range(DA_HEADS):
        sl = slice(h * LANES, (h + 1) * LANES)
        q = q_ref[0, :, sl]
        kp = kp_ref[0, :, sl].astype(BF16)
        kn = kn_ref[0, :, sl].astype(BF16)
        zero = jnp.zeros_like(q)
        wp, wn = [], []
        for c in range(2):
            qc = jnp.where((lane >= DA_HEAD_DIM) if c else (lane < DA_HEAD_DIM), q, zero)
            sp = jnp.where(vis_p, _dot_nt(qc, kp), NEG_BIG)
            sn = jnp.where(vis_n, _dot_nt(qc, kn), NEG_BIG)
            m = jnp.maximum(jnp.max(sp, axis=-1, keepdims=True), jnp.max(sn, axis=-1, keepdims=True))
            pp = jnp.exp2(sp - m)
            pn = jnp.exp2(sn - m)
            inv = 1.0 / (jnp.sum(pp, axis=-1, keepdims=True) + jnp.sum(pn, axis=-1, keepdims=True))
            wp.append(pp * inv)
            wn.append(pn * inv)
        wgt_p = (wp[0] - lam * wp[1]).astype(BF16)
        wgt_n = (wn[0] - lam * wn[1]).astype(BF16)
        o = _dot(wgt_p, vp_ref[0, :, sl].astype(BF16)) + _dot(wgt_n, vn_ref[0, :, sl].astype(BF16))
        o = o * lax.rsqrt(jnp.mean(o * o, axis=-1, keepdims=True) + EPS)
        o = o * (sg_ref[...] * (1.0 - LAMBDA_INIT))
        o_ref[0, :, sl] = (o * gd_ref[0, :, sl].astype(F32)).astype(BF16)


def _attn_sample(q, k_new, v_new, k_past, v_past, gd, subln_g, lam_params):
    b, ts, _ = q.shape
    past = k_past.shape[1]
    new = pl.BlockSpec((1, ts, DA_WIDTH), lambda i: (i, 0, 0))
    old = pl.BlockSpec((1, past, DA_WIDTH), lambda i: (i, 0, 0))
    return pl.pallas_call(
        functools.partial(_attn_sample_kernel, past),
        grid=(b,),
        in_specs=[new, new, new, old, old, new,
                  pl.BlockSpec((1, DA_VDIM), lambda i: (0, 0)),
                  pl.BlockSpec((4, DA_HEAD_DIM), lambda i: (0, 0))],
        out_specs=new,
        out_shape=jax.ShapeDtypeStruct((b, ts, DA_WIDTH), BF16),
        compiler_params=pltpu.CompilerParams(
            dimension_semantics=("arbitrary",), vmem_limit_bytes=VMEM_LIMIT_BYTES),
        name="diff_attn_sample",
    )(q, k_new, v_new, k_past, v_past, gd, subln_g, lam_params)


def _out_proj_kernel(x_ref, a_ref, mix_ref, w_ref, y_ref):
    y_ref[0] = (x_ref[0] + _dot(a_ref[0], w_ref[:DA_WIDTH, :]) + _dot(mix_ref[0], w_ref[DA_WIDTH:, :]))


def _out_proj(x, out_a, mix, w_out, tm):
    b, t, _ = x.shape
    tile = lambda width: pl.BlockSpec((1, tm, width), lambda i, j: (i, j, 0))
    return pl.pallas_call(
        _out_proj_kernel,
        grid=(b, t // tm),
        in_specs=[tile(D_MODEL), tile(DA_WIDTH), tile(DA_WIDTH),
                  pl.BlockSpec((D_MODEL, D_MODEL), lambda i, j: (0, 0))],
        out_specs=tile(D_MODEL),
        out_shape=jax.ShapeDtypeStruct((b, t, D_MODEL), F32),
        compiler_params=pltpu.CompilerParams(
            dimension_semantics=("arbitrary", "arbitrary"), vmem_limit_bytes=VMEM_LIMIT_BYTES),
        name="out_proj",
    )(x, out_a, mix, w_out)


def _rope_tables(pos):
    half = ROT_DIM // 2
    inv = ROPE_THETA ** (-jnp.arange(0, ROT_DIM, 2, dtype=F32) / ROT_DIM)
    ang = pos.astype(F32)[:, None] * inv[None, :]
    cos, sin = jnp.cos(ang), jnp.sin(ang)
    t = pos.shape[0]
    ones = jnp.ones((t, DA_HEAD_DIM - ROT_DIM), F32)
    zeros = jnp.zeros((t, DA_HEAD_DIM - ROT_DIM), F32)
    zh = jnp.zeros((t, half), F32)
    c = jnp.concatenate([cos, cos, ones], axis=1)
    s_up = jnp.concatenate([-sin, zh, zeros], axis=1)
    s_dn = jnp.concatenate([zh, sin, zeros], axis=1)
    rep = LANES // DA_HEAD_DIM
    return tuple(jnp.tile(a, (1, rep)) for a in (c, s_up, s_dn))


def _block_diag(w):
    n, bi, bj = w.shape
    eye = jnp.eye(n, dtype=w.dtype)
    return (eye[:, None, :, None] * w[:, :, None, :]).reshape(n * bi, n * bj)


def _pick_tile(t, pref):
    return pref if t % pref == 0 else t


def kernel(x_prompt, x_sample, mem_prompt, cache_diff_k, cache_diff_v, cache_mem_k, cache_mem_v,
           state_lru_conv, state_lru_h, norm_g, w_in, da_q_norm_g, da_k_norm_g, lambda_q1, lambda_k1,
           lambda_q2, lambda_k2, da_subln_g, lru_conv_w, lru_conv_b, lru_w_a, lru_b_a, lru_w_x, lru_b_x,
           lru_lambda, mem_norm_g, w_mem_kv, mx_q_norm_g, mx_k_norm_g, w_out):
    depth = w_in.shape[0]
    assert depth == 1
    bp, tp, _ = x_prompt.shape
    bs, ts, _ = x_sample.shape
    past = cache_diff_k.shape[2]
    l = 0
    grp = np.arange(DA_QK) // DA_HEAD_DIM
    ones512 = jnp.asarray(grp[:, None] == grp[None, :], BF16)
    wts = dict(
        w_in=w_in[l].astype(BF16), norm_g=norm_g[l][None, :],
        gq=jnp.tile(da_q_norm_g[l], DA_QK // DA_HEAD_DIM)[None, :],
        gk=jnp.tile(da_k_norm_g[l], DA_QK // DA_HEAD_DIM)[None, :],
        gmq=jnp.tile(mx_q_norm_g[l], MX_HEADS)[None, :], ones512=ones512,
        conv_w=lru_conv_w[l], conv_b=lru_conv_b[l][None, :],
        wa=_block_diag(lru_w_a[l]).astype(BF16), ba=lru_b_a[l][None, :],
        wx=_block_diag(lru_w_x[l]).astype(BF16), bx=lru_b_x[l][None, :],
        lru_lambda=lru_lambda[l][None, :])
    lam_params = jnp.stack([lambda_q1[l], lambda_k1[l], lambda_q2[l], lambda_k2[l]])
    subln_g = da_subln_g[l][None, :]
    w_out_b = w_out[l].astype(BF16)

    mk_p, mv_p = _mem_kv(mem_prompt, mem_norm_g[l][None, :], w_mem_kv[l].astype(BF16),
                         jnp.tile(mx_k_norm_g[l], MX_HEADS)[None, :], ones512[:MX_WIDTH, :MX_WIDTH])
    tm_p = _pick_tile(tp, 512)
    tq_p = _pick_tile(tp, 512)
    pos_p = jnp.arange(tp, dtype=jnp.int32)
    q_p, k_p, v_p, gd_p, mix_p, c_p, h_p, kb_p, vt_p = _in_proj(
        x_prompt, jnp.zeros((bp, CONV_W - 1, LRU_WIDTH), F32), jnp.zeros((bp, 1, LRU_WIDTH), F32),
        mk_p, mv_p, wts, _rope_tables(pos_p), tm_p, True)
    oa_p = _attn_prompt(q_p, kb_p, vt_p, gd_p, subln_g, lam_params, tq_p, tm_p)
    y_p = _out_proj(x_prompt, oa_p, mix_p, w_out_b, tm_p)

    pos_s = past + jnp.arange(ts, dtype=jnp.int32)
    q_s, k_s, v_s, gd_s, mix_s, c_s, h_s = _in_proj(
        x_sample, state_lru_conv[l], state_lru_h[l][:, None, :],
        cache_mem_k[l].reshape(bs, N_MEM, MX_WIDTH), cache_mem_v[l].reshape(bs, N_MEM, MX_WIDTH),
        wts, _rope_tables(pos_s), ts, False)
    oa_s = _attn_sample(q_s, k_s, v_s, cache_diff_k[l].reshape(bs, past, DA_QK),
                        cache_diff_v[l].reshape(bs, past, DA_WIDTH), gd_s, subln_g, lam_params)
    y_s = _out_proj(x_sample, oa_s, mix_s, w_out_b, ts)

    return (y_p, y_s,
            k_p.reshape(1, bp, tp, DA_HEADS, 2, DA_HEAD_DIM), v_p.reshape(1, bp, tp, DA_HEADS, DA_VDIM),
            mk_p.reshape(1, bp, N_MEM, MX_HEADS, MX_HEAD_DIM), mv_p.reshape(1, bp, N_MEM, MX_HEADS, MX_HEAD_DIM),
            c_p[None], h_p.reshape(1, bp, LRU_WIDTH),
            k_s.reshape(1, bs, ts, DA_HEADS, 2, DA_HEAD_DIM), v_s.reshape(1, bs, ts, DA_HEADS, DA_VDIM),
            c_s[None], h_s.reshape(1, bs, LRU_WIDTH))
```

```python
import functools
import math

import numpy as np
import jax
import jax.numpy as jnp
from jax import lax
from jax.experimental import pallas as pl
from jax.experimental.pallas import tpu as pltpu

D_MODEL = 1024
CHUNK = 64
EPS = 1e-6
DA_HEADS = 4
DA_HEAD_DIM = 64
DA_VDIM = 128
DA_QK = 512
DA_WIDTH = 512
ROPE_THETA = 500000.0
ROT_DIM = 16
LRU_WIDTH = 256
LRU_BLOCKS = 4
CONV_W = 4
LRU_C = 8.0
MX_HEADS = 4
MX_HEAD_DIM = 64
MX_WIDTH = 256
N_MEM = 256
LAMBDA_INIT = 0.8 - 0.6 * math.exp(-0.3 * 0)
NEG_BIG = -1e30
LOG2E = math.log2(math.e)

OFF_DQ, OFF_DK, OFF_DV, OFF_DG = 0, 512, 1024, 1536
OFF_LX, OFF_LG, OFF_MQ, OFF_MG = 2048, 2304, 2560, 2816
IN_WIDTH = 3072

VMEM_LIMIT_BYTES = 52 * 1024 * 1024
LANES = 128
MXU_TILE = 256
CONV_PAD = 8
ATTN_HEADS_PER_STEP = 4
VT_ONES = 16

F32 = jnp.float32
BF16 = jnp.bfloat16


def _dot(a, b):
    return jnp.dot(a, b, preferred_element_type=F32)


def _dot_nt(a, b):
    return lax.dot_general(a, b, (((1,), (1,)), ((), ())), preferred_element_type=F32)


def _sigmoid(x):
    return 1.0 / (1.0 + jnp.exp(-x))


def _silu(x):
    return x * _sigmoid(x)


def _group_rms_scale(x, ones_bd, group):
    ss = _dot((x * x).astype(BF16), ones_bd)
    return lax.rsqrt(ss * (1.0 / group) + EPS)


def _rope(x, c, s_up, s_dn):
    w = x.shape[1]
    return x * c + pltpu.roll(x, w - ROT_DIM // 2, 1) * s_up + pltpu.roll(x, ROT_DIM // 2, 1) * s_dn


def _mem_kv_kernel(mem_ref, g_ref, w_ref, gk_ref, ones_ref, k_ref, v_ref):
    x = mem_ref[0]
    y = x * lax.rsqrt(jnp.mean(x * x, axis=-1, keepdims=True) + EPS) * g_ref[...]
    kv = _dot(y.astype(BF16), w_ref[...])
    k = kv[:, :MX_WIDTH]
    k_ref[0] = k * _group_rms_scale(k, ones_ref[...], MX_HEAD_DIM) * gk_ref[...]
    v_ref[0] = kv[:, MX_WIDTH:]


def _mem_kv(mem, mem_norm_g, w_mem_kv, gk_tiled, ones256):
    b, n, _ = mem.shape
    full = lambda shape: pl.BlockSpec(shape, lambda i: (0,) * len(shape))
    return pl.pallas_call(
        _mem_kv_kernel,
        grid=(b,),
        in_specs=[
            pl.BlockSpec((1, n, D_MODEL), lambda i: (i, 0, 0)),
            full((1, D_MODEL)),
            full((D_MODEL, 2 * MX_WIDTH)),
            full((1, MX_WIDTH)),
            full((MX_WIDTH, MX_WIDTH)),
        ],
        out_specs=[pl.BlockSpec((1, n, MX_WIDTH), lambda i: (i, 0, 0))] * 2,
        out_shape=[jax.ShapeDtypeStruct((b, n, MX_WIDTH), F32)] * 2,
        compiler_params=pltpu.CompilerParams(
            dimension_semantics=("arbitrary",), vmem_limit_bytes=VMEM_LIMIT_BYTES),
        name="mem_kv",
    )(mem, mem_norm_g, w_mem_kv, gk_tiled, ones256)


def _lru_scan(a, b):
    tm = a.shape[0]
    row = lax.broadcasted_iota(jnp.int32, a.shape, 0)
    d = 1
    while d < tm:
        keep = row >= d
        b = a * jnp.where(keep, pltpu.roll(b, d, 0), 0.0) + b
        a = a * jnp.where(keep, pltpu.roll(a, d, 0), 1.0)
        d *= 2
    return b


def _in_proj_kernel(emit_attn_layouts, tm,
                    x_ref, cbuf_ref, h0_ref, w_ref, ng_ref, gq_ref, gk_ref, gmq_ref, ones_ref,
                    rc_ref, ru_ref, rd_ref, cw_ref, cb_ref, wa_ref, ba_ref, wx_ref, bx_ref,
                    lam_ref, mk_ref, mv_ref, *rest):
    if emit_attn_layouts:
        (q_ref, k_ref, v_ref, gd_ref, mix_ref, cout_ref, hout_ref, kb_ref, vt_ref,
         xbuf, hc) = rest
    else:
        q_ref, k_ref, v_ref, gd_ref, mix_ref, cout_ref, hout_ref, xbuf, hc = rest
    t = pl.program_id(1)

    @pl.when(t == 0)
    def _():
        xbuf[pl.ds(CONV_PAD - (CONV_W - 1), CONV_W - 1), :] = cbuf_ref[0]
        hc[...] = h0_ref[0]

    x = x_ref[0]
    hn = (x * lax.rsqrt(jnp.mean(x * x, axis=-1, keepdims=True) + EPS) * ng_ref[...]).astype(BF16)

    def proj(off, width):
        return _dot(hn, w_ref[:, off:off + width])

    rc = jnp.concatenate([rc_ref[...]] * (DA_QK // LANES), axis=1)
    ru = jnp.concatenate([ru_ref[...]] * (DA_QK // LANES), axis=1)
    rd = jnp.concatenate([rd_ref[...]] * (DA_QK // LANES), axis=1)
    ones512 = ones_ref[...]

    dq = proj(OFF_DQ, DA_QK)
    q = _rope(dq * _group_rms_scale(dq, ones512, DA_HEAD_DIM) * gq_ref[...], rc, ru, rd)
    q_ref[0] = (q * (DA_HEAD_DIM ** -0.5 * LOG2E)).astype(BF16)
    dk = proj(OFF_DK, DA_QK)
    k = _rope(dk * _group_rms_scale(dk, ones512, DA_HEAD_DIM) * gk_ref[...], rc, ru, rd)
    k_ref[0] = k
    dv = proj(OFF_DV, DA_WIDTH)
    v_ref[0] = dv
    if emit_attn_layouts:
        kb_ref[0] = k.astype(BF16)
        for h in range(DA_HEADS):
            vt_ref[0, h, 0, :DA_VDIM, :] = dv[:, h * DA_VDIM:(h + 1) * DA_VDIM].T.astype(BF16)
            vt_ref[0, h, 0, DA_VDIM:, :] = jnp.ones((VT_ONES, tm), BF16)
    gd_ref[0] = _silu(proj(OFF_DG, DA_WIDTH)).astype(BF16)

    lx = proj(OFF_LX, LRU_WIDTH)
    xbuf[pl.ds(CONV_PAD, tm), :] = lx
    xc = cb_ref[...]
    for j in range(CONV_W):
        xc = xc + xbuf[pl.ds(CONV_PAD - (CONV_W - 1) + j, tm), :] * cw_ref[j:j + 1, :]
    tail = xbuf[pl.ds(CONV_PAD + tm - (CONV_W - 1), CONV_W - 1), :]
    cout_ref[0] = tail
    xbuf[pl.ds(CONV_PAD - (CONV_W - 1), CONV_W - 1), :] = tail
    xcb = xc.astype(BF16)
    r = _sigmoid(_dot(xcb, wa_ref[...]) + ba_ref[...])
    i = _sigmoid(_dot(xcb, wx_ref[...]) + bx_ref[...])
    neg_lam = -lam_ref[...]
    softplus = jnp.maximum(neg_lam, 0.0) + jnp.log1p(jnp.exp(-jnp.abs(neg_lam)))
    log_a = (-LRU_C) * r * softplus
    a = jnp.exp(log_a)
    one_m_a = jnp.where(a == 1.0, -log_a, jnp.where(a == 0.0, 1.0, (1.0 - a) * log_a / jnp.log(a)))
    bb = jnp.sqrt(one_m_a * (1.0 + a)) * (i * xc)
    row = lax.broadcasted_iota(jnp.int32, a.shape, 0)
    bb = bb + jnp.where(row == 0, a * hc[...], 0.0)
    hseq = _lru_scan(a, bb)
    h_last = hseq[tm - 1:tm, :]
    hc[...] = h_last
    hout_ref[0] = h_last
    out_b = hseq * _silu(proj(OFF_LG, LRU_WIDTH))

    mq = proj(OFF_MQ, MX_WIDTH)
    qm = (mq * _group_rms_scale(mq, ones512[:MX_WIDTH, :MX_WIDTH], MX_HEAD_DIM) * gmq_ref[...]
          * (MX_HEAD_DIM ** -0.5)).astype(BF16)
    mk = mk_ref[0].astype(BF16)
    mv = mv_ref[0].astype(BF16)
    lane_q = lax.broadcasted_iota(jnp.int32, (tm, LANES), 1)
    lane_v = lax.broadcasted_iota(jnp.int32, (N_MEM, LANES), 1)
    slabs = []
    for sl in range(MX_WIDTH // LANES):
        qs = qm[:, sl * LANES:(sl + 1) * LANES]
        ks = mk[:, sl * LANES:(sl + 1) * LANES]
        vs = mv[:, sl * LANES:(sl + 1) * LANES]
        acc = None
        rl = None
        for half in range(2):
            sel_q = (lane_q >= MX_HEAD_DIM) if half else (lane_q < MX_HEAD_DIM)
            sel_v = (lane_v >= MX_HEAD_DIM) if half else (lane_v < MX_HEAD_DIM)
            s = _dot_nt(jnp.where(sel_q, qs, jnp.zeros_like(qs)), ks)
            p = jnp.exp(s - jnp.max(s, axis=-1, keepdims=True))
            rsum = 1.0 / jnp.sum(p, axis=-1, keepdims=True)
            o = _dot(p.astype(BF16), jnp.where(sel_v, vs, jnp.zeros_like(vs)))
            acc = o if acc is None else acc + o
            rl = rsum if rl is None else jnp.where(sel_q, rsum, rl)
        slabs.append(acc * rl)
    om = jnp.concatenate(slabs, axis=1)
    out_c = om * _silu(proj(OFF_MG, MX_WIDTH))
    mix_ref[0] = jnp.concatenate([out_b, out_c], axis=1).astype(BF16)


def _in_proj(x, cbuf, h0, mk, mv, wts, rope_tabs, tm, emit_attn_layouts):
    b, t, _ = x.shape
    nt = t // tm
    full = lambda shape: pl.BlockSpec(shape, lambda i, j: (0,) * len(shape))
    per_b = lambda shape: pl.BlockSpec(shape, lambda i, j: (i,) + (0,) * (len(shape) - 1))
    tile = lambda width: pl.BlockSpec((1, tm, width), lambda i, j: (i, j, 0))
    rope_spec = pl.BlockSpec((tm, LANES), lambda i, j: (j, 0))
    in_specs = [
        tile(D_MODEL), per_b((1, CONV_W - 1, LRU_WIDTH)), per_b((1, 1, LRU_WIDTH)),
        full((D_MODEL, IN_WIDTH)), full((1, D_MODEL)), full((1, DA_QK)), full((1, DA_QK)),
        full((1, MX_WIDTH)), full((DA_QK, DA_QK)),
        rope_spec, rope_spec, rope_spec,
        full((CONV_W, LRU_WIDTH)), full((1, LRU_WIDTH)),
        full((LRU_WIDTH, LRU_WIDTH)), full((1, LRU_WIDTH)),
        full((LRU_WIDTH, LRU_WIDTH)), full((1, LRU_WIDTH)), full((1, LRU_WIDTH)),
        per_b((1, N_MEM, MX_WIDTH)), per_b((1, N_MEM, MX_WIDTH)),
    ]
    out_specs = [tile(DA_QK), tile(DA_QK), tile(DA_WIDTH), tile(DA_WIDTH), tile(DA_WIDTH),
                 per_b((1, CONV_W - 1, LRU_WIDTH)), per_b((1, 1, LRU_WIDTH))]
    out_shape = [jax.ShapeDtypeStruct((b, t, DA_QK), BF16),
                 jax.ShapeDtypeStruct((b, t, DA_QK), F32),
                 jax.ShapeDtypeStruct((b, t, DA_WIDTH), F32),
                 jax.ShapeDtypeStruct((b, t, DA_WIDTH), BF16),
                 jax.ShapeDtypeStruct((b, t, DA_WIDTH), BF16),
                 jax.ShapeDtypeStruct((b, CONV_W - 1, LRU_WIDTH), F32),
                 jax.ShapeDtypeStruct((b, 1, LRU_WIDTH), F32)]
    if emit_attn_layouts:
        out_specs += [tile(DA_QK),
                      pl.BlockSpec((1, DA_HEADS, 1, DA_VDIM + VT_ONES, tm), lambda i, j: (i, 0, j, 0, 0))]
        out_shape += [jax.ShapeDtypeStruct((b, t, DA_QK), BF16),
                      jax.ShapeDtypeStruct((b, DA_HEADS, nt, DA_VDIM + VT_ONES, tm), BF16)]
    return pl.pallas_call(
        functools.partial(_in_proj_kernel, emit_attn_layouts, tm),
        grid=(b, nt),
        in_specs=in_specs,
        out_specs=out_specs,
        out_shape=out_shape,
        scratch_shapes=[pltpu.VMEM((CONV_PAD + tm, LRU_WIDTH), F32), pltpu.VMEM((1, LRU_WIDTH), F32)],
        compiler_params=pltpu.CompilerParams(
            dimension_semantics=("arbitrary", "arbitrary"), vmem_limit_bytes=VMEM_LIMIT_BYTES),
        name="in_proj_prompt" if emit_attn_layouts else "in_proj_sample",
    )(x, cbuf, h0, wts["w_in"], wts["norm_g"], wts["gq"], wts["gk"], wts["gmq"], wts["ones512"],
      *rope_tabs, wts["conv_w"], wts["conv_b"], wts["wa"], wts["ba"], wts["wx"], wts["bx"],
      wts["lru_lambda"], mk, mv)


def _lambda_value(lam_ref):
    lp = lam_ref[...]
    s1 = jnp.sum(lp[0:1] * lp[1:2], axis=-1, keepdims=True)
    s2 = jnp.sum(lp[2:3] * lp[3:4], axis=-1, keepdims=True)
    return jnp.exp(s1) - jnp.exp(s2) + LAMBDA_INIT


def _attn_prompt_kernel(tq, tk, hb, q_ref, qn_ref, k_ref, feat_ref, vt_ref, gd_ref, sg_ref, lam_ref, o_ref,
                        w_scr, s_scr, tmax_scr, m_scr, acc_scr):
    qi = pl.program_id(2)
    cw = 2 * tq
    lane = lax.broadcasted_iota(jnp.int32, (tq, LANES), 1)
    row = lax.broadcasted_iota(jnp.int32, (tq, LANES), 0)

    def build_w(slot, src_ref, q_tile):
        qfeat = jnp.where(lane > ((q_tile * tq + row) >> 6), NEG_BIG, 0.0).astype(BF16)
        for h in range(hb):
            q = src_ref[0, :, h * LANES:(h + 1) * LANES]
            zero = jnp.zeros_like(q)
            w_scr[slot, h] = jnp.concatenate(
                [jnp.concatenate([jnp.where(lane < DA_HEAD_DIM, q, zero), qfeat], axis=1),
                 jnp.concatenate([jnp.where(lane >= DA_HEAD_DIM, q, zero), qfeat], axis=1)], axis=0)

    build_w(0, q_ref, qi)
    build_w(1, qn_ref, qi + 1)
    n = ((qi + 1) * tq + tk - 1) // tk

    def key_rows(j, h):
        off = pl.multiple_of(j * tk, tk)
        return jnp.concatenate([k_ref[0, pl.ds(off, tk), h * LANES:(h + 1) * LANES],
                                feat_ref[pl.ds(off, tk), :]], axis=1)

    def stage_scores(kx, slot, h, cc):
        cols = slice(h * cw + cc * MXU_TILE, h * cw + (cc + 1) * MXU_TILE)
        s = _dot_nt(kx, w_scr[slot, h, cc * MXU_TILE:(cc + 1) * MXU_TILE, :])
        s_scr[:, cols] = s
        tmax_scr[:, cols] = jnp.max(s, axis=0, keepdims=True)

    @pl.when(qi == 0)
    def _():
        for h in range(hb):
            kx = key_rows(0, h)
            for cc in range(cw // MXU_TILE):
                stage_scores(kx, 0, h, cc)

    m_scr[...] = jnp.full(m_scr.shape, NEG_BIG, F32)
    acc_scr[...] = jnp.zeros(acc_scr.shape, F32)

    def body(j, c):
        m_old = m_scr[...]
        m_new = jnp.maximum(m_old, tmax_scr[...])
        alpha = jnp.exp2(m_old - m_new)
        m_scr[...] = m_new
        last = j + 1 >= n
        j_next = jnp.where(last, 0, j + 1)
        slot = jnp.where(last, 1, 0)
        for h in range(hb):
            kx = key_rows(j_next, h)
            for cc in range(cw // MXU_TILE):
                cols = slice(h * cw + cc * MXU_TILE, h * cw + (cc + 1) * MXU_TILE)
                acc_c = alpha[:, cols] * acc_scr[:, cols]
                for r in range(tk // MXU_TILE):
                    rows = slice(r * MXU_TILE, (r + 1) * MXU_TILE)
                    p = jnp.exp2(s_scr[rows, cols] - m_new[:, cols]).astype(BF16)
                    acc_c = acc_c + _dot(vt_ref[0, h, j, :, rows], p)
                acc_scr[:, cols] = acc_c
                stage_scores(kx, slot, h, cc)
        return c

    lax.fori_loop(0, n, body, 0)

    acc = acc_scr[...]
    acc = acc[:DA_VDIM] * (1.0 / acc[DA_VDIM:DA_VDIM + 1])
    lam = _lambda_value(lam_ref)
    for h in range(hb):
        ot = acc[:, h * cw:h * cw + tq] - lam * acc[:, h * cw + tq:(h + 1) * cw]
        ot = ot * lax.rsqrt(jnp.mean(ot * ot, axis=0, keepdims=True) + EPS)
        o = ot.T * (sg_ref[...] * (1.0 - LAMBDA_INIT))
        sl = slice(h * LANES, (h + 1) * LANES)
        o_ref[0, :, sl] = (o * gd_ref[0, :, sl].astype(F32)).astype(BF16)


def _attn_prompt(q, kb, vt, gd, subln_g, lam_params, tq, tk, hb):
    b, t, _ = q.shape
    nk, nq = t // tk, t // tq
    assert t % CHUNK == 0 and t // CHUNK <= LANES, "chunk one-hot must fit one 128-lane slab"
    assert DA_HEADS % hb == 0
    key_chunk = np.arange(t) // CHUNK
    feat = jnp.asarray(key_chunk[:, None] == np.arange(LANES)[None, :], BF16)
    hw = hb * LANES
    vrows = DA_VDIM + VT_ONES
    once = dict(pipeline_mode=pl.Buffered(1))
    return pl.pallas_call(
        functools.partial(_attn_prompt_kernel, tq, tk, hb),
        grid=(b, DA_HEADS // hb, nq),
        in_specs=[
            pl.BlockSpec((1, tq, hw), lambda i, h, j: (i, j, h)),
            pl.BlockSpec((1, tq, hw), lambda i, h, j: (i, jnp.minimum(j + 1, nq - 1), h)),
            pl.BlockSpec((1, t, hw), lambda i, h, j: (i, 0, h), **once),
            pl.BlockSpec((t, LANES), lambda i, h, j: (0, 0), **once),
            pl.BlockSpec((1, hb, nk, vrows, tk), lambda i, h, j: (i, h, 0, 0, 0), **once),
            pl.BlockSpec((1, tq, hw), lambda i, h, j: (i, j, h)),
            pl.BlockSpec((1, DA_VDIM), lambda i, h, j: (0, 0)),
            pl.BlockSpec((4, DA_HEAD_DIM), lambda i, h, j: (0, 0)),
        ],
        out_specs=pl.BlockSpec((1, tq, hw), lambda i, h, j: (i, j, h)),
        out_shape=jax.ShapeDtypeStruct((b, t, DA_WIDTH), BF16),
        scratch_shapes=[pltpu.VMEM((2, hb, 2 * tq, 2 * LANES), BF16),
                        pltpu.VMEM((tk, hb * 2 * tq), F32),
                        pltpu.VMEM((1, hb * 2 * tq), F32), pltpu.VMEM((1, hb * 2 * tq), F32),
                        pltpu.VMEM((vrows, hb * 2 * tq), F32)],
        compiler_params=pltpu.CompilerParams(
            dimension_semantics=("arbitrary", "arbitrary", "arbitrary"),
            vmem_limit_bytes=VMEM_LIMIT_BYTES),
        name="diff_attn_prompt",
    )(q, q, kb, feat, vt, gd, subln_g, lam_params)


def _attn_sample_kernel(past, q_ref, kn_ref, vn_ref, kp_ref, vp_ref, gd_ref, sg_ref, lam_ref, o_ref):
    ts = q_ref.shape[1]
    lam = _lambda_value(lam_ref)
    lane = lax.broadcasted_iota(jnp.int32, (ts, LANES), 1)
    qrow = past + lax.broadcasted_iota(jnp.int32, (ts, past), 0)
    vis_p = (lax.broadcasted_iota(jnp.int32, (ts, past), 1) >> 6) <= (qrow >> 6)
    qrow_n = past + lax.broadcasted_iota(jnp.int32, (ts, ts), 0)
    vis_n = ((past + lax.broadcasted_iota(jnp.int32, (ts, ts), 1)) >> 6) <= (qrow_n >> 6)
    for h in range(DA_HEADS):
        sl = slice(h * LANES, (h + 1) * LANES)
        q = q_ref[0, :, sl]
        kp = kp_ref[0, :, sl].astype(BF16)
        kn = kn_ref[0, :, sl].astype(BF16)
        zero = jnp.zeros_like(q)
        wp, wn = [], []
        for c in range(2):
            qc = jnp.where((lane >= DA_HEAD_DIM) if c else (lane < DA_HEAD_DIM), q, zero)
            sp = jnp.where(vis_p, _dot_nt(qc, kp), NEG_BIG)
            sn = jnp.where(vis_n, _dot_nt(qc, kn), NEG_BIG)
            m = jnp.maximum(jnp.max(sp, axis=-1, keepdims=True), jnp.max(sn, axis=-1, keepdims=True))
            pp = jnp.exp2(sp - m)
            pn = jnp.exp2(sn - m)
            inv = 1.0 / (jnp.sum(pp, axis=-1, keepdims=True) + jnp.sum(pn, axis=-1, keepdims=True))
            wp.append(pp * inv)
            wn.append(pn * inv)
        wgt_p = (wp[0] - lam * wp[1]).astype(BF16)
        wgt_n = (wn[0] - lam * wn[1]).astype(BF16)
        o = _dot(wgt_p, vp_ref[0, :, sl].astype(BF16)) + _dot(wgt_n, vn_ref[0, :, sl].astype(BF16))
        o = o * lax.rsqrt(jnp.mean(o * o, axis=-1, keepdims=True) + EPS)
        o = o * (sg_ref[...] * (1.0 - LAMBDA_INIT))
        o_ref[0, :, sl] = (o * gd_ref[0, :, sl].astype(F32)).astype(BF16)


def _attn_sample(q, k_new, v_new, k_past, v_past, gd, subln_g, lam_params):
    b, ts, _ = q.shape
    past = k_past.shape[1]
    new = pl.BlockSpec((1, ts, DA_WIDTH), lambda i: (i, 0, 0))
    old = pl.BlockSpec((1, past, DA_WIDTH), lambda i: (i, 0, 0))
    return pl.pallas_call(
        functools.partial(_attn_sample_kernel, past),
        grid=(b,),
        in_specs=[new, new, new, old, old, new,
                  pl.BlockSpec((1, DA_VDIM), lambda i: (0, 0)),
                  pl.BlockSpec((4, DA_HEAD_DIM), lambda i: (0, 0))],
        out_specs=new,
        out_shape=jax.ShapeDtypeStruct((b, ts, DA_WIDTH), BF16),
        compiler_params=pltpu.CompilerParams(
            dimension_semantics=("arbitrary",), vmem_limit_bytes=VMEM_LIMIT_BYTES),
        name="diff_attn_sample",
    )(q, k_new, v_new, k_past, v_past, gd, subln_g, lam_params)


def _out_proj_kernel(x_ref, a_ref, mix_ref, w_ref, y_ref):
    y_ref[0] = (x_ref[0] + _dot(a_ref[0], w_ref[:DA_WIDTH, :]) + _dot(mix_ref[0], w_ref[DA_WIDTH:, :]))


def _out_proj(x, out_a, mix, w_out, tm):
    b, t, _ = x.shape
    tile = lambda width: pl.BlockSpec((1, tm, width), lambda i, j: (i, j, 0))
    return pl.pallas_call(
        _out_proj_kernel,
        grid=(b, t // tm),
        in_specs=[tile(D_MODEL), tile(DA_WIDTH), tile(DA_WIDTH),
                  pl.BlockSpec((D_MODEL, D_MODEL), lambda i, j: (0, 0))],
        out_specs=tile(D_MODEL),
        out_shape=jax.ShapeDtypeStruct((b, t, D_MODEL), F32),
        compiler_params=pltpu.CompilerParams(
            dimension_semantics=("arbitrary", "arbitrary"), vmem_limit_bytes=VMEM_LIMIT_BYTES),
        name="out_proj",
    )(x, out_a, mix, w_out)


def _rope_tables(pos):
    half = ROT_DIM // 2
    inv = ROPE_THETA ** (-jnp.arange(0, ROT_DIM, 2, dtype=F32) / ROT_DIM)
    ang = pos.astype(F32)[:, None] * inv[None, :]
    cos, sin = jnp.cos(ang), jnp.sin(ang)
    t = pos.shape[0]
    ones = jnp.ones((t, DA_HEAD_DIM - ROT_DIM), F32)
    zeros = jnp.zeros((t, DA_HEAD_DIM - ROT_DIM), F32)
    zh = jnp.zeros((t, half), F32)
    c = jnp.concatenate([cos, cos, ones], axis=1)
    s_up = jnp.concatenate([-sin, zh, zeros], axis=1)
    s_dn = jnp.concatenate([zh, sin, zeros], axis=1)
    rep = LANES // DA_HEAD_DIM
    return tuple(jnp.tile(a, (1, rep)) for a in (c, s_up, s_dn))


def _block_diag(w):
    n, bi, bj = w.shape
    eye = jnp.eye(n, dtype=w.dtype)
    return (eye[:, None, :, None] * w[:, :, None, :]).reshape(n * bi, n * bj)


def _pick_tile(t, pref):
    return pref if t % pref == 0 else t


def kernel(x_prompt, x_sample, mem_prompt, cache_diff_k, cache_diff_v, cache_mem_k, cache_mem_v,
           state_lru_conv, state_lru_h, norm_g, w_in, da_q_norm_g, da_k_norm_g, lambda_q1, lambda_k1,
           lambda_q2, lambda_k2, da_subln_g, lru_conv_w, lru_conv_b, lru_w_a, lru_b_a, lru_w_x, lru_b_x,
           lru_lambda, mem_norm_g, w_mem_kv, mx_q_norm_g, mx_k_norm_g, w_out):
    depth = w_in.shape[0]
    assert depth == 1
    bp, tp, _ = x_prompt.shape
    bs, ts, _ = x_sample.shape
    past = cache_diff_k.shape[2]
    l = 0
    grp = np.arange(DA_QK) // DA_HEAD_DIM
    ones512 = jnp.asarray(grp[:, None] == grp[None, :], BF16)
    wts = dict(
        w_in=w_in[l].astype(BF16), norm_g=norm_g[l][None, :],
        gq=jnp.tile(da_q_norm_g[l], DA_QK // DA_HEAD_DIM)[None, :],
        gk=jnp.tile(da_k_norm_g[l], DA_QK // DA_HEAD_DIM)[None, :],
        gmq=jnp.tile(mx_q_norm_g[l], MX_HEADS)[None, :], ones512=ones512,
        conv_w=lru_conv_w[l], conv_b=lru_conv_b[l][None, :],
        wa=_block_diag(lru_w_a[l]).astype(BF16), ba=lru_b_a[l][None, :],
        wx=_block_diag(lru_w_x[l]).astype(BF16), bx=lru_b_x[l][None, :],
        lru_lambda=lru_lambda[l][None, :])
    lam_params = jnp.stack([lambda_q1[l], lambda_k1[l], lambda_q2[l], lambda_k2[l]])
    subln_g = da_subln_g[l][None, :]
    w_out_b = w_out[l].astype(BF16)

    mk_p, mv_p = _mem_kv(mem_prompt, mem_norm_g[l][None, :], w_mem_kv[l].astype(BF16),
                         jnp.tile(mx_k_norm_g[l], MX_HEADS)[None, :], ones512[:MX_WIDTH, :MX_WIDTH])
    tm_p = _pick_tile(tp, 512)
    tq_p = _pick_tile(tp, 512)
    pos_p = jnp.arange(tp, dtype=jnp.int32)
    q_p, k_p, v_p, gd_p, mix_p, c_p, h_p, kb_p, vt_p = _in_proj(
        x_prompt, jnp.zeros((bp, CONV_W - 1, LRU_WIDTH), F32), jnp.zeros((bp, 1, LRU_WIDTH), F32),
        mk_p, mv_p, wts, _rope_tables(pos_p), tm_p, True)
    oa_p = _attn_prompt(q_p, kb_p, vt_p, gd_p, subln_g, lam_params, tq_p, tm_p, ATTN_HEADS_PER_STEP)
    y_p = _out_proj(x_prompt, oa_p, mix_p, w_out_b, tm_p)

    pos_s = past + jnp.arange(ts, dtype=jnp.int32)
    q_s, k_s, v_s, gd_s, mix_s, c_s, h_s = _in_proj(
        x_sample, state_lru_conv[l], state_lru_h[l][:, None, :],
        cache_mem_k[l].reshape(bs, N_MEM, MX_WIDTH), cache_mem_v[l].reshape(bs, N_MEM, MX_WIDTH),
        wts, _rope_tables(pos_s), ts, False)
    oa_s = _attn_sample(q_s, k_s, v_s, cache_diff_k[l].reshape(bs, past, DA_QK),
                        cache_diff_v[l].reshape(bs, past, DA_WIDTH), gd_s, subln_g, lam_params)
    y_s = _out_proj(x_sample, oa_s, mix_s, w_out_b, ts)

    return (y_p, y_s,
            k_p.reshape(1, bp, tp, DA_HEADS, 2, DA_HEAD_DIM), v_p.reshape(1, bp, tp, DA_HEADS, DA_VDIM),
            mk_p.reshape(1, bp, N_MEM, MX_HEADS, MX_HEAD_DIM), mv_p.reshape(1, bp, N_MEM, MX_HEADS, MX_HEAD_DIM),
            c_p[None], h_p.reshape(1, bp, LRU_WIDTH),
            k_s.reshape(1, bs, ts, DA_HEADS, 2, DA_HEAD_DIM), v_s.reshape(1, bs, ts, DA_HEADS, DA_VDIM),
            c_s[None], h_s.reshape(1, bs, LRU_WIDTH))
```

```python
import functools
import math

import numpy as np
import jax
import jax.numpy as jnp
from jax import lax
from jax.experimental import pallas as pl
from jax.experimental.pallas import tpu as pltpu

D_MODEL = 1024
CHUNK = 64
EPS = 1e-6
DA_HEADS = 4
DA_HEAD_DIM = 64
DA_VDIM = 128
DA_QK = 512
DA_WIDTH = 512
ROPE_THETA = 500000.0
ROT_DIM = 16
LRU_WIDTH = 256
LRU_BLOCKS = 4
CONV_W = 4
LRU_C = 8.0
MX_HEADS = 4
MX_HEAD_DIM = 64
MX_WIDTH = 256
N_MEM = 256
LAMBDA_INIT = 0.8 - 0.6 * math.exp(-0.3 * 0)
NEG_BIG = -1e30
LOG2E = math.log2(math.e)

OFF_DQ, OFF_DK, OFF_DV, OFF_DG = 0, 512, 1024, 1536
OFF_LX, OFF_LG, OFF_MQ, OFF_MG = 2048, 2304, 2560, 2816
IN_WIDTH = 3072

VMEM_LIMIT_BYTES = 52 * 1024 * 1024
LANES = 128
MXU_TILE = 256
CONV_PAD = 8
ATTN_HEADS_PER_STEP = 4
VT_ONES = 16

F32 = jnp.float32
BF16 = jnp.bfloat16


def _dot(a, b):
    return jnp.dot(a, b, preferred_element_type=F32)


def _dot_nt(a, b):
    return lax.dot_general(a, b, (((1,), (1,)), ((), ())), preferred_element_type=F32)


def _sigmoid(x):
    return 0.5 + 0.5 * jnp.tanh(0.5 * x)


def _silu(x):
    return x * _sigmoid(x)


def _group_rms_scale(x, ones_bd, group):
    xsq = (x * x).astype(BF16)
    wb = ones_bd.shape[0]
    ss = jnp.concatenate([_dot(xsq[:, c:c + wb], ones_bd) for c in range(0, x.shape[1], wb)], axis=1)
    return lax.rsqrt(ss * (1.0 / group) + EPS)


def _rope(x, c, s_up, s_dn):
    w = x.shape[1]
    return x * c + pltpu.roll(x, w - ROT_DIM // 2, 1) * s_up + pltpu.roll(x, ROT_DIM // 2, 1) * s_dn


def _mem_kv_kernel(mem_ref, g_ref, w_ref, gk_ref, ones_ref, k_ref, v_ref):
    x = mem_ref[0]
    y = x * lax.rsqrt(jnp.mean(x * x, axis=-1, keepdims=True) + EPS) * g_ref[...]
    kv = _dot(y.astype(BF16), w_ref[...])
    k = kv[:, :MX_WIDTH]
    k_ref[0] = k * _group_rms_scale(k, ones_ref[...], MX_HEAD_DIM) * gk_ref[...]
    v_ref[0] = kv[:, MX_WIDTH:]


def _mem_kv(mem, mem_norm_g, w_mem_kv, gk_tiled, ones256):
    b, n, _ = mem.shape
    full = lambda shape: pl.BlockSpec(shape, lambda i: (0,) * len(shape))
    return pl.pallas_call(
        _mem_kv_kernel,
        grid=(b,),
        in_specs=[
            pl.BlockSpec((1, n, D_MODEL), lambda i: (i, 0, 0)),
            full((1, D_MODEL)),
            full((D_MODEL, 2 * MX_WIDTH)),
            full((1, MX_WIDTH)),
            full((MX_WIDTH, MX_WIDTH)),
        ],
        out_specs=[pl.BlockSpec((1, n, MX_WIDTH), lambda i: (i, 0, 0))] * 2,
        out_shape=[jax.ShapeDtypeStruct((b, n, MX_WIDTH), F32)] * 2,
        compiler_params=pltpu.CompilerParams(
            dimension_semantics=("arbitrary",), vmem_limit_bytes=VMEM_LIMIT_BYTES),
        name="mem_kv",
    )(mem, mem_norm_g, w_mem_kv, gk_tiled, ones256)


def _lru_scan(a, b):
    tm = a.shape[0]
    row = lax.broadcasted_iota(jnp.int32, a.shape, 0)
    d = 1
    while d < tm:
        keep = row >= d
        b = a * jnp.where(keep, pltpu.roll(b, d, 0), 0.0) + b
        a = a * jnp.where(keep, pltpu.roll(a, d, 0), 1.0)
        d *= 2
    return b


def _in_proj_kernel(emit_attn_layouts, tm,
                    x_ref, cbuf_ref, h0_ref, w_ref, ng_ref, gq_ref, gk_ref, gmq_ref, ones_ref,
                    rc_ref, ru_ref, rd_ref, cw_ref, cb_ref, wa_ref, ba_ref, wx_ref, bx_ref,
                    lam_ref, mk_ref, mv_ref, *rest):
    if emit_attn_layouts:
        (q_ref, k_ref, v_ref, gd_ref, mix_ref, cout_ref, hout_ref, kb_ref, vt_ref,
         xbuf, hc) = rest
    else:
        q_ref, k_ref, v_ref, gd_ref, mix_ref, cout_ref, hout_ref, xbuf, hc = rest
    t = pl.program_id(1)

    @pl.when(t == 0)
    def _():
        xbuf[pl.ds(CONV_PAD - (CONV_W - 1), CONV_W - 1), :] = cbuf_ref[0]
        hc[...] = h0_ref[0]

    x = x_ref[0]
    hn = (x * lax.rsqrt(jnp.mean(x * x, axis=-1, keepdims=True) + EPS) * ng_ref[...]).astype(BF16)

    def proj(off, width):
        return _dot(hn, w_ref[:, off:off + width])

    rc = jnp.concatenate([rc_ref[...]] * (DA_QK // LANES), axis=1)
    ru = jnp.concatenate([ru_ref[...]] * (DA_QK // LANES), axis=1)
    rd = jnp.concatenate([rd_ref[...]] * (DA_QK // LANES), axis=1)
    ones_bd = ones_ref[...]

    dq = proj(OFF_DQ, DA_QK)
    q = _rope(dq * _group_rms_scale(dq, ones_bd, DA_HEAD_DIM) * gq_ref[...], rc, ru, rd)
    q_ref[0] = (q * (DA_HEAD_DIM ** -0.5 * LOG2E)).astype(BF16)
    dk = proj(OFF_DK, DA_QK)
    k = _rope(dk * _group_rms_scale(dk, ones_bd, DA_HEAD_DIM) * gk_ref[...], rc, ru, rd)
    k_ref[0] = k
    dv = proj(OFF_DV, DA_WIDTH)
    for h in range(DA_HEADS):
        v_ref[0, pl.ds(h, tm, stride=DA_HEADS), :] = dv[:, h * DA_VDIM:(h + 1) * DA_VDIM]
    if emit_attn_layouts:
        kb_ref[0] = k.astype(BF16)
        for h in range(DA_HEADS):
            vt_ref[0, h, 0, :DA_VDIM, :] = dv[:, h * DA_VDIM:(h + 1) * DA_VDIM].T.astype(BF16)
            vt_ref[0, h, 0, DA_VDIM:, :] = jnp.ones((VT_ONES, tm), BF16)
    gd_ref[0] = _silu(proj(OFF_DG, DA_WIDTH)).astype(BF16)

    lx = proj(OFF_LX, LRU_WIDTH)
    xbuf[pl.ds(CONV_PAD, tm), :] = lx
    xc = cb_ref[...]
    for j in range(CONV_W):
        xc = xc + xbuf[pl.ds(CONV_PAD - (CONV_W - 1) + j, tm), :] * cw_ref[j:j + 1, :]
    tail = xbuf[pl.ds(CONV_PAD + tm - (CONV_W - 1), CONV_W - 1), :]
    cout_ref[0] = tail
    xbuf[pl.ds(CONV_PAD - (CONV_W - 1), CONV_W - 1), :] = tail
    xcb = xc.astype(BF16)
    r = _sigmoid(_dot(xcb, wa_ref[...]) + ba_ref[...])
    i = _sigmoid(_dot(xcb, wx_ref[...]) + bx_ref[...])
    neg_lam = -lam_ref[...]
    softplus = jnp.maximum(neg_lam, 0.0) + jnp.log1p(jnp.exp(-jnp.abs(neg_lam)))
    log_a = (-LRU_C) * r * softplus
    a = jnp.exp(log_a)
    bb = (1.0 + a) * jnp.sqrt(-jnp.tanh(0.5 * log_a)) * (i * xc)
    row = lax.broadcasted_iota(jnp.int32, a.shape, 0)
    bb = bb + jnp.where(row == 0, a * hc[...], 0.0)
    hseq = _lru_scan(a, bb)
    h_last = hseq[tm - 1:tm, :]
    hc[...] = h_last
    hout_ref[0] = h_last
    out_b = hseq * _silu(proj(OFF_LG, LRU_WIDTH))

    mq = proj(OFF_MQ, MX_WIDTH)
    qm = (mq * _group_rms_scale(mq, ones_bd, MX_HEAD_DIM) * gmq_ref[...]
          * (MX_HEAD_DIM ** -0.5)).astype(BF16)
    mk = mk_ref[0].astype(BF16)
    mv = mv_ref[0].astype(BF16)
    lane_q = lax.broadcasted_iota(jnp.int32, (tm, LANES), 1)
    lane_v = lax.broadcasted_iota(jnp.int32, (N_MEM, LANES), 1)
    slabs = []
    for sl in range(MX_WIDTH // LANES):
        qs = qm[:, sl * LANES:(sl + 1) * LANES]
        ks = mk[:, sl * LANES:(sl + 1) * LANES]
        vs = mv[:, sl * LANES:(sl + 1) * LANES]
        acc = None
        rl = None
        for half in range(2):
            sel_q = (lane_q >= MX_HEAD_DIM) if half else (lane_q < MX_HEAD_DIM)
            sel_v = (lane_v >= MX_HEAD_DIM) if half else (lane_v < MX_HEAD_DIM)
            s = _dot_nt(jnp.where(sel_q, qs, jnp.zeros_like(qs)), ks)
            p = jnp.exp(s - jnp.max(s, axis=-1, keepdims=True))
            rsum = 1.0 / jnp.sum(p, axis=-1, keepdims=True)
            o = _dot(p.astype(BF16), jnp.where(sel_v, vs, jnp.zeros_like(vs)))
            acc = o if acc is None else acc + o
            rl = rsum if rl is None else jnp.where(sel_q, rsum, rl)
        slabs.append(acc * rl)
    om = jnp.concatenate(slabs, axis=1)
    out_c = om * _silu(proj(OFF_MG, MX_WIDTH))
    mix_ref[0] = jnp.concatenate([out_b, out_c], axis=1).astype(BF16)


def _in_proj(x, cbuf, h0, mk, mv, wts, rope_tabs, tm, emit_attn_layouts):
    b, t, _ = x.shape
    nt = t // tm
    full = lambda shape: pl.BlockSpec(shape, lambda i, j: (0,) * len(shape))
    per_b = lambda shape: pl.BlockSpec(shape, lambda i, j: (i,) + (0,) * (len(shape) - 1))
    tile = lambda width: pl.BlockSpec((1, tm, width), lambda i, j: (i, j, 0))
    rope_spec = pl.BlockSpec((tm, LANES), lambda i, j: (j, 0))
    in_specs = [
        tile(D_MODEL), per_b((1, CONV_W - 1, LRU_WIDTH)), per_b((1, 1, LRU_WIDTH)),
        full((D_MODEL, IN_WIDTH)), full((1, D_MODEL)), full((1, DA_QK)), full((1, DA_QK)),
        full((1, MX_WIDTH)), full((MXU_TILE, MXU_TILE)),
        rope_spec, rope_spec, rope_spec,
        full((CONV_W, LRU_WIDTH)), full((1, LRU_WIDTH)),
        full((LRU_WIDTH, LRU_WIDTH)), full((1, LRU_WIDTH)),
        full((LRU_WIDTH, LRU_WIDTH)), full((1, LRU_WIDTH)), full((1, LRU_WIDTH)),
        per_b((1, N_MEM, MX_WIDTH)), per_b((1, N_MEM, MX_WIDTH)),
    ]
    out_specs = [tile(DA_QK), tile(DA_QK),
                 pl.BlockSpec((1, tm * DA_HEADS, DA_VDIM), lambda i, j: (i, j, 0)),
                 tile(DA_WIDTH), tile(DA_WIDTH),
                 per_b((1, CONV_W - 1, LRU_WIDTH)), per_b((1, 1, LRU_WIDTH))]
    out_shape = [jax.ShapeDtypeStruct((b, t, DA_QK), BF16),
                 jax.ShapeDtypeStruct((b, t, DA_QK), F32),
                 jax.ShapeDtypeStruct((b, t * DA_HEADS, DA_VDIM), F32),
                 jax.ShapeDtypeStruct((b, t, DA_WIDTH), BF16),
                 jax.ShapeDtypeStruct((b, t, DA_WIDTH), BF16),
                 jax.ShapeDtypeStruct((b, CONV_W - 1, LRU_WIDTH), F32),
                 jax.ShapeDtypeStruct((b, 1, LRU_WIDTH), F32)]
    if emit_attn_layouts:
        out_specs += [tile(DA_QK),
                      pl.BlockSpec((1, DA_HEADS, 1, DA_VDIM + VT_ONES, tm), lambda i, j: (i, 0, j, 0, 0))]
        out_shape += [jax.ShapeDtypeStruct((b, t, DA_QK), BF16),
                      jax.ShapeDtypeStruct((b, DA_HEADS, nt, DA_VDIM + VT_ONES, tm), BF16)]
    return pl.pallas_call(
        functools.partial(_in_proj_kernel, emit_attn_layouts, tm),
        grid=(b, nt),
        in_specs=in_specs,
        out_specs=out_specs,
        out_shape=out_shape,
        scratch_shapes=[pltpu.VMEM((CONV_PAD + tm, LRU_WIDTH), F32), pltpu.VMEM((1, LRU_WIDTH), F32)],
        compiler_params=pltpu.CompilerParams(
            dimension_semantics=("arbitrary", "arbitrary"), vmem_limit_bytes=VMEM_LIMIT_BYTES),
        name="in_proj_prompt" if emit_attn_layouts else "in_proj_sample",
    )(x, cbuf, h0, wts["w_in"], wts["norm_g"], wts["gq"], wts["gk"], wts["gmq"], wts["ones_bd"],
      *rope_tabs, wts["conv_w"], wts["conv_b"], wts["wa"], wts["ba"], wts["wx"], wts["bx"],
      wts["lru_lambda"], mk, mv)


def _lambda_value(lam_ref):
    lp = lam_ref[...]
    s1 = jnp.sum(lp[0:1] * lp[1:2], axis=-1, keepdims=True)
    s2 = jnp.sum(lp[2:3] * lp[3:4], axis=-1, keepdims=True)
    return jnp.exp(s1) - jnp.exp(s2) + LAMBDA_INIT


def _attn_prompt_kernel(tq, tk, hb, q_ref, qn_ref, k_ref, feat_ref, vt_ref, gd_ref, sg_ref, lam_ref, o_ref,
                        w_scr, s_scr, tmax_scr, m_scr, acc_scr):
    qi = pl.program_id(2)
    cw = 2 * tq
    lane = lax.broadcasted_iota(jnp.int32, (tq, LANES), 1)
    row = lax.broadcasted_iota(jnp.int32, (tq, LANES), 0)

    def build_w(slot, src_ref, q_tile):
        qfeat = jnp.where(lane > ((q_tile * tq + row) >> 6), NEG_BIG, 0.0).astype(BF16)
        for h in range(hb):
            q = src_ref[0, :, h * LANES:(h + 1) * LANES]
            zero = jnp.zeros_like(q)
            w_scr[slot, h] = jnp.concatenate(
                [jnp.concatenate([jnp.where(lane < DA_HEAD_DIM, q, zero), qfeat], axis=1),
                 jnp.concatenate([jnp.where(lane >= DA_HEAD_DIM, q, zero), qfeat], axis=1)], axis=0)

    build_w(0, q_ref, qi)
    build_w(1, qn_ref, qi + 1)
    n = ((qi + 1) * tq + tk - 1) // tk

    def key_rows(j, h):
        off = pl.multiple_of(j * tk, tk)
        return jnp.concatenate([k_ref[0, pl.ds(off, tk), h * LANES:(h + 1) * LANES],
                                feat_ref[pl.ds(off, tk), :]], axis=1)

    def stage_scores(kx, slot, h, cc):
        cols = slice(h * cw + cc * MXU_TILE, h * cw + (cc + 1) * MXU_TILE)
        s = _dot_nt(kx, w_scr[slot, h, cc * MXU_TILE:(cc + 1) * MXU_TILE, :])
        s_scr[:, cols] = s
        tmax_scr[:, cols] = jnp.max(s, axis=0, keepdims=True)

    @pl.when(qi == 0)
    def _():
        for h in range(hb):
            kx = key_rows(0, h)
            for cc in range(cw // MXU_TILE):
                stage_scores(kx, 0, h, cc)

    m_scr[...] = jnp.full(m_scr.shape, NEG_BIG, F32)
    acc_scr[...] = jnp.zeros(acc_scr.shape, F32)

    def body(j, c):
        m_old = m_scr[...]
        m_new = jnp.maximum(m_old, tmax_scr[...])
        alpha = jnp.exp2(m_old - m_new)
        m_scr[...] = m_new
        last = j + 1 >= n
        j_next = jnp.where(last, 0, j + 1)
        slot = jnp.where(last, 1, 0)
        for h in range(hb):
            kx = key_rows(j_next, h)
            for cc in range(cw // MXU_TILE):
                cols = slice(h * cw + cc * MXU_TILE, h * cw + (cc + 1) * MXU_TILE)
                acc_c = alpha[:, cols] * acc_scr[:, cols]
                for r in range(tk // MXU_TILE):
                    rows = slice(r * MXU_TILE, (r + 1) * MXU_TILE)
                    p = jnp.exp2(s_scr[rows, cols] - m_new[:, cols]).astype(BF16)
                    acc_c = acc_c + _dot(vt_ref[0, h, j, :, rows], p)
                acc_scr[:, cols] = acc_c
                stage_scores(kx, slot, h, cc)
        return c

    lax.fori_loop(0, n, body, 0)

    acc = acc_scr[...]
    acc = acc[:DA_VDIM] * (1.0 / acc[DA_VDIM:DA_VDIM + 1])
    lam = _lambda_value(lam_ref)
    for h in range(hb):
        ot = acc[:, h * cw:h * cw + tq] - lam * acc[:, h * cw + tq:(h + 1) * cw]
        ot = ot * lax.rsqrt(jnp.mean(ot * ot, axis=0, keepdims=True) + EPS)
        o = ot.T * (sg_ref[...] * (1.0 - LAMBDA_INIT))
        sl = slice(h * LANES, (h + 1) * LANES)
        o_ref[0, :, sl] = (o * gd_ref[0, :, sl].astype(F32)).astype(BF16)


def _attn_prompt(q, kb, vt, gd, subln_g, lam_params, tq, tk, hb):
    b, t, _ = q.shape
    nk, nq = t // tk, t // tq
    assert t % CHUNK == 0 and t // CHUNK <= LANES, "chunk one-hot must fit one 128-lane slab"
    assert DA_HEADS % hb == 0
    key_chunk = np.arange(t) // CHUNK
    feat = jnp.asarray(key_chunk[:, None] == np.arange(LANES)[None, :], BF16)
    hw = hb * LANES
    vrows = DA_VDIM + VT_ONES
    once = dict(pipeline_mode=pl.Buffered(1))
    return pl.pallas_call(
        functools.partial(_attn_prompt_kernel, tq, tk, hb),
        grid=(b, DA_HEADS // hb, nq),
        in_specs=[
            pl.BlockSpec((1, tq, hw), lambda i, h, j: (i, j, h)),
            pl.BlockSpec((1, tq, hw), lambda i, h, j: (i, jnp.minimum(j + 1, nq - 1), h)),
            pl.BlockSpec((1, t, hw), lambda i, h, j: (i, 0, h), **once),
            pl.BlockSpec((t, LANES), lambda i, h, j: (0, 0), **once),
            pl.BlockSpec((1, hb, nk, vrows, tk), lambda i, h, j: (i, h, 0, 0, 0), **once),
            pl.BlockSpec((1, tq, hw), lambda i, h, j: (i, j, h)),
            pl.BlockSpec((1, DA_VDIM), lambda i, h, j: (0, 0)),
            pl.BlockSpec((4, DA_HEAD_DIM), lambda i, h, j: (0, 0)),
        ],
        out_specs=pl.BlockSpec((1, tq, hw), lambda i, h, j: (i, j, h)),
        out_shape=jax.ShapeDtypeStruct((b, t, DA_WIDTH), BF16),
        scratch_shapes=[pltpu.VMEM((2, hb, 2 * tq, 2 * LANES), BF16),
                        pltpu.VMEM((tk, hb * 2 * tq), F32),
                        pltpu.VMEM((1, hb * 2 * tq), F32), pltpu.VMEM((1, hb * 2 * tq), F32),
                        pltpu.VMEM((vrows, hb * 2 * tq), F32)],
        compiler_params=pltpu.CompilerParams(
            dimension_semantics=("arbitrary", "arbitrary", "arbitrary"),
            vmem_limit_bytes=VMEM_LIMIT_BYTES),
        name="diff_attn_prompt",
    )(q, q, kb, feat, vt, gd, subln_g, lam_params)


def _attn_sample_kernel(past, q_ref, kn_ref, vn_ref, kp_ref, vp_ref, gd_ref, sg_ref, lam_ref, o_ref):
    ts = q_ref.shape[1]
    lam = _lambda_value(lam_ref)
    lane = lax.broadcasted_iota(jnp.int32, (ts, LANES), 1)
    qrow = past + lax.broadcasted_iota(jnp.int32, (ts, past), 0)
    vis_p = (lax.broadcasted_iota(jnp.int32, (ts, past), 1) >> 6) <= (qrow >> 6)
    qrow_n = past + lax.broadcasted_iota(jnp.int32, (ts, ts), 0)
    vis_n = ((past + lax.broadcasted_iota(jnp.int32, (ts, ts), 1)) >> 6) <= (qrow_n >> 6)
    for h in range(DA_HEADS):
        sl = slice(h * LANES, (h + 1) * LANES)
        q = q_ref[0, :, sl]
        kp_t = kp_ref[0, sl, :].astype(BF16)
        kn = kn_ref[0, :, sl].astype(BF16)
        zero = jnp.zeros_like(q)
        wp, wn = [], []
        for c in range(2):
            qc = jnp.where((lane >= DA_HEAD_DIM) if c else (lane < DA_HEAD_DIM), q, zero)
            sp = jnp.where(vis_p, _dot(qc, kp_t), NEG_BIG)
            sn = jnp.where(vis_n, _dot_nt(qc, kn), NEG_BIG)
            m = jnp.maximum(jnp.max(sp, axis=-1, keepdims=True), jnp.max(sn, axis=-1, keepdims=True))
            pp = jnp.exp2(sp - m)
            pn = jnp.exp2(sn - m)
            inv = 1.0 / (jnp.sum(pp, axis=-1, keepdims=True) + jnp.sum(pn, axis=-1, keepdims=True))
            wp.append(pp * inv)
            wn.append(pn * inv)
        wgt_p = (wp[0] - lam * wp[1]).astype(BF16)
        wgt_n = (wn[0] - lam * wn[1]).astype(BF16)
        v_past = vp_ref[0, pl.ds(h, past, stride=DA_HEADS), :].astype(BF16)
        v_new = vn_ref[0, pl.ds(h, ts, stride=DA_HEADS), :].astype(BF16)
        o = _dot(wgt_p, v_past) + _dot(wgt_n, v_new)
        o = o * lax.rsqrt(jnp.mean(o * o, axis=-1, keepdims=True) + EPS)
        o = o * (sg_ref[...] * (1.0 - LAMBDA_INIT))
        o_ref[0, :, sl] = (o * gd_ref[0, :, sl].astype(F32)).astype(BF16)


def _attn_sample(q, k_new, v_new, k_past_t, v_past, gd, subln_g, lam_params):
    b, ts, _ = q.shape
    past = k_past_t.shape[2]
    new = pl.BlockSpec((1, ts, DA_WIDTH), lambda i: (i, 0, 0))
    new_v = pl.BlockSpec((1, ts * DA_HEADS, DA_VDIM), lambda i: (i, 0, 0))
    old_kt = pl.BlockSpec((1, DA_QK, past), lambda i: (i, 0, 0))
    old_v = pl.BlockSpec((1, past * DA_HEADS, DA_VDIM), lambda i: (i, 0, 0))
    return pl.pallas_call(
        functools.partial(_attn_sample_kernel, past),
        grid=(b,),
        in_specs=[new, new, new_v, old_kt, old_v, new,
                  pl.BlockSpec((1, DA_VDIM), lambda i: (0, 0)),
                  pl.BlockSpec((4, DA_HEAD_DIM), lambda i: (0, 0))],
        out_specs=new,
        out_shape=jax.ShapeDtypeStruct((b, ts, DA_WIDTH), BF16),
        compiler_params=pltpu.CompilerParams(
            dimension_semantics=("arbitrary",), vmem_limit_bytes=VMEM_LIMIT_BYTES),
        name="diff_attn_sample",
    )(q, k_new, v_new, k_past_t, v_past, gd, subln_g, lam_params)


def _out_proj_kernel(x_ref, a_ref, mix_ref, w_ref, y_ref):
    y_ref[0] = (x_ref[0] + _dot(a_ref[0], w_ref[:DA_WIDTH, :]) + _dot(mix_ref[0], w_ref[DA_WIDTH:, :]))


def _out_proj(x, out_a, mix, w_out, tm):
    b, t, _ = x.shape
    tile = lambda width: pl.BlockSpec((1, tm, width), lambda i, j: (i, j, 0))
    return pl.pallas_call(
        _out_proj_kernel,
        grid=(b, t // tm),
        in_specs=[tile(D_MODEL), tile(DA_WIDTH), tile(DA_WIDTH),
                  pl.BlockSpec((D_MODEL, D_MODEL), lambda i, j: (0, 0))],
        out_specs=tile(D_MODEL),
        out_shape=jax.ShapeDtypeStruct((b, t, D_MODEL), F32),
        compiler_params=pltpu.CompilerParams(
            dimension_semantics=("arbitrary", "arbitrary"), vmem_limit_bytes=VMEM_LIMIT_BYTES),
        name="out_proj",
    )(x, out_a, mix, w_out)


def _rope_tables(pos):
    half = ROT_DIM // 2
    inv = ROPE_THETA ** (-jnp.arange(0, ROT_DIM, 2, dtype=F32) / ROT_DIM)
    ang = pos.astype(F32)[:, None] * inv[None, :]
    cos, sin = jnp.cos(ang), jnp.sin(ang)
    t = pos.shape[0]
    ones = jnp.ones((t, DA_HEAD_DIM - ROT_DIM), F32)
    zeros = jnp.zeros((t, DA_HEAD_DIM - ROT_DIM), F32)
    zh = jnp.zeros((t, half), F32)
    c = jnp.concatenate([cos, cos, ones], axis=1)
    s_up = jnp.concatenate([-sin, zh, zeros], axis=1)
    s_dn = jnp.concatenate([zh, sin, zeros], axis=1)
    rep = LANES // DA_HEAD_DIM
    return tuple(jnp.tile(a, (1, rep)) for a in (c, s_up, s_dn))


def _block_diag(w):
    n, bi, bj = w.shape
    eye = jnp.eye(n, dtype=w.dtype)
    return (eye[:, None, :, None] * w[:, :, None, :]).reshape(n * bi, n * bj)


def _pick_tile(t, pref):
    return pref if t % pref == 0 else t


def kernel(x_prompt, x_sample, mem_prompt, cache_diff_k, cache_diff_v, cache_mem_k, cache_mem_v,
           state_lru_conv, state_lru_h, norm_g, w_in, da_q_norm_g, da_k_norm_g, lambda_q1, lambda_k1,
           lambda_q2, lambda_k2, da_subln_g, lru_conv_w, lru_conv_b, lru_w_a, lru_b_a, lru_w_x, lru_b_x,
           lru_lambda, mem_norm_g, w_mem_kv, mx_q_norm_g, mx_k_norm_g, w_out):
    depth = w_in.shape[0]
    assert depth == 1
    bp, tp, _ = x_prompt.shape
    bs, ts, _ = x_sample.shape
    past = cache_diff_k.shape[2]
    l = 0
    grp = np.arange(MXU_TILE) // DA_HEAD_DIM
    ones_bd = jnp.asarray(grp[:, None] == grp[None, :], BF16)
    wts = dict(
        w_in=w_in[l].astype(BF16), norm_g=norm_g[l][None, :],
        gq=jnp.tile(da_q_norm_g[l], DA_QK // DA_HEAD_DIM)[None, :],
        gk=jnp.tile(da_k_norm_g[l], DA_QK // DA_HEAD_DIM)[None, :],
        gmq=jnp.tile(mx_q_norm_g[l], MX_HEADS)[None, :], ones_bd=ones_bd,
        conv_w=lru_conv_w[l], conv_b=lru_conv_b[l][None, :],
        wa=_block_diag(lru_w_a[l]).astype(BF16), ba=lru_b_a[l][None, :],
        wx=_block_diag(lru_w_x[l]).astype(BF16), bx=lru_b_x[l][None, :],
        lru_lambda=lru_lambda[l][None, :])
    lam_params = jnp.stack([lambda_q1[l], lambda_k1[l], lambda_q2[l], lambda_k2[l]])
    subln_g = da_subln_g[l][None, :]
    w_out_b = w_out[l].astype(BF16)

    mk_p, mv_p = _mem_kv(mem_prompt, mem_norm_g[l][None, :], w_mem_kv[l].astype(BF16),
                         jnp.tile(mx_k_norm_g[l], MX_HEADS)[None, :], ones_bd)
    tm_p = _pick_tile(tp, 512)
    tq_p = _pick_tile(tp, 512)
    pos_p = jnp.arange(tp, dtype=jnp.int32)
    q_p, k_p, v_p, gd_p, mix_p, c_p, h_p, kb_p, vt_p = _in_proj(
        x_prompt, jnp.zeros((bp, CONV_W - 1, LRU_WIDTH), F32), jnp.zeros((bp, 1, LRU_WIDTH), F32),
        mk_p, mv_p, wts, _rope_tables(pos_p), tm_p, True)
    oa_p = _attn_prompt(q_p, kb_p, vt_p, gd_p, subln_g, lam_params, tq_p, tm_p, ATTN_HEADS_PER_STEP)
    y_p = _out_proj(x_prompt, oa_p, mix_p, w_out_b, tm_p)

    pos_s = past + jnp.arange(ts, dtype=jnp.int32)
    q_s, k_s, v_s, gd_s, mix_s, c_s, h_s = _in_proj(
        x_sample, state_lru_conv[l], state_lru_h[l][:, None, :],
        cache_mem_k[l].reshape(bs, N_MEM, MX_WIDTH), cache_mem_v[l].reshape(bs, N_MEM, MX_WIDTH),
        wts, _rope_tables(pos_s), ts, False)
    oa_s = _attn_sample(q_s, k_s, v_s, jnp.swapaxes(cache_diff_k[l].reshape(bs, past, DA_QK), 1, 2),
                        cache_diff_v[l].reshape(bs, past * DA_HEADS, DA_VDIM), gd_s, subln_g, lam_params)
    y_s = _out_proj(x_sample, oa_s, mix_s, w_out_b, ts)

    return (y_p, y_s,
            k_p.reshape(1, bp, tp, DA_HEADS, 2, DA_HEAD_DIM), v_p.reshape(1, bp, tp, DA_HEADS, DA_VDIM),
            mk_p.reshape(1, bp, N_MEM, MX_HEADS, MX_HEAD_DIM), mv_p.reshape(1, bp, N_MEM, MX_HEADS, MX_HEAD_DIM),
            c_p[None], h_p.reshape(1, bp, LRU_WIDTH),
            k_s.reshape(1, bs, ts, DA_HEADS, 2, DA_HEAD_DIM), v_s.reshape(1, bs, ts, DA_HEADS, DA_VDIM),
            c_s[None], h_s.reshape(1, bs, LRU_WIDTH))
```

```python
import functools
import math

import numpy as np
import jax
import jax.numpy as jnp
from jax import lax
from jax.experimental import pallas as pl
from jax.experimental.pallas import tpu as pltpu

D_MODEL = 1024
CHUNK = 64
EPS = 1e-6
DA_HEADS = 4
DA_HEAD_DIM = 64
DA_VDIM = 128
DA_QK = 512
DA_WIDTH = 512
ROPE_THETA = 500000.0
ROT_DIM = 16
LRU_WIDTH = 256
LRU_BLOCKS = 4
CONV_W = 4
LRU_C = 8.0
MX_HEADS = 4
MX_HEAD_DIM = 64
MX_WIDTH = 256
N_MEM = 256
LAMBDA_INIT = 0.8 - 0.6 * math.exp(-0.3 * 0)
NEG_BIG = -1e30
LOG2E = math.log2(math.e)

OFF_DQ, OFF_DK, OFF_DV, OFF_DG = 0, 512, 1024, 1536
OFF_LX, OFF_LG, OFF_MQ, OFF_MG = 2048, 2304, 2560, 2816
IN_WIDTH = 3072

VMEM_LIMIT_BYTES = 52 * 1024 * 1024
LANES = 128
MXU_TILE = 256
CONV_PAD = 8
ATTN_HEADS_PER_STEP = 4
VT_ONES = 16

F32 = jnp.float32
BF16 = jnp.bfloat16


def _dot(a, b):
    return jnp.dot(a, b, preferred_element_type=F32)


def _dot_nt(a, b):
    return lax.dot_general(a, b, (((1,), (1,)), ((), ())), preferred_element_type=F32)


def _sigmoid(x):
    return 0.5 + 0.5 * jnp.tanh(0.5 * x)


def _silu(x):
    return x * _sigmoid(x)


def _group_rms_scale(x, ones_bd, group):
    xsq = (x * x).astype(BF16)
    wb = ones_bd.shape[0]
    ss = jnp.concatenate([_dot(xsq[:, c:c + wb], ones_bd) for c in range(0, x.shape[1], wb)], axis=1)
    return lax.rsqrt(ss * (1.0 / group) + EPS)


def _rope(x, c, s_up, s_dn):
    w = x.shape[1]
    return x * c + pltpu.roll(x, w - ROT_DIM // 2, 1) * s_up + pltpu.roll(x, ROT_DIM // 2, 1) * s_dn


def _mem_kv_kernel(mem_ref, g_ref, w_ref, gk_ref, ones_ref, k_ref, v_ref):
    x = mem_ref[0]
    y = x * lax.rsqrt(jnp.mean(x * x, axis=-1, keepdims=True) + EPS) * g_ref[...]
    kv = _dot(y.astype(BF16), w_ref[...])
    k = kv[:, :MX_WIDTH]
    k_ref[0] = k * _group_rms_scale(k, ones_ref[...], MX_HEAD_DIM) * gk_ref[...]
    v_ref[0] = kv[:, MX_WIDTH:]


def _mem_kv(mem, mem_norm_g, w_mem_kv, gk_tiled, ones256):
    b, n, _ = mem.shape
    full = lambda shape: pl.BlockSpec(shape, lambda i: (0,) * len(shape))
    return pl.pallas_call(
        _mem_kv_kernel,
        grid=(b,),
        in_specs=[
            pl.BlockSpec((1, n, D_MODEL), lambda i: (i, 0, 0)),
            full((1, D_MODEL)),
            full((D_MODEL, 2 * MX_WIDTH)),
            full((1, MX_WIDTH)),
            full((MX_WIDTH, MX_WIDTH)),
        ],
        out_specs=[pl.BlockSpec((1, n, MX_WIDTH), lambda i: (i, 0, 0))] * 2,
        out_shape=[jax.ShapeDtypeStruct((b, n, MX_WIDTH), F32)] * 2,
        compiler_params=pltpu.CompilerParams(
            dimension_semantics=("arbitrary",), vmem_limit_bytes=VMEM_LIMIT_BYTES),
        name="mem_kv",
    )(mem, mem_norm_g, w_mem_kv, gk_tiled, ones256)


def _lru_scan(a, b):
    tm = a.shape[0]
    row = lax.broadcasted_iota(jnp.int32, a.shape, 0)
    d = 1
    while d < tm:
        keep = row >= d
        b = a * jnp.where(keep, pltpu.roll(b, d, 0), 0.0) + b
        a = a * jnp.where(keep, pltpu.roll(a, d, 0), 1.0)
        d *= 2
    return b


def _in_proj_kernel(emit_attn_layouts, tm,
                    x_ref, cbuf_ref, h0_ref, w_ref, ng_ref, gq_ref, gk_ref, gmq_ref, ones_ref,
                    rc_ref, ru_ref, rd_ref, cw_ref, cb_ref, wa_ref, ba_ref, wx_ref, bx_ref,
                    lam_ref, mk_ref, mv_ref, *rest):
    if emit_attn_layouts:
        (q_ref, k_ref, v_ref, gd_ref, mix_ref, cout_ref, hout_ref, kb_ref, vt_ref,
         xbuf, hc) = rest
    else:
        q_ref, k_ref, v_ref, gd_ref, mix_ref, cout_ref, hout_ref, xbuf, hc = rest
    t = pl.program_id(1)

    @pl.when(t == 0)
    def _():
        xbuf[pl.ds(CONV_PAD - (CONV_W - 1), CONV_W - 1), :] = cbuf_ref[0]
        hc[...] = h0_ref[0]

    x = x_ref[0]
    hn = (x * lax.rsqrt(jnp.mean(x * x, axis=-1, keepdims=True) + EPS) * ng_ref[...]).astype(BF16)

    def proj(off, width):
        return _dot(hn, w_ref[:, off:off + width])

    rc = jnp.concatenate([rc_ref[...]] * (DA_QK // LANES), axis=1)
    ru = jnp.concatenate([ru_ref[...]] * (DA_QK // LANES), axis=1)
    rd = jnp.concatenate([rd_ref[...]] * (DA_QK // LANES), axis=1)
    ones_bd = ones_ref[...]

    dq = proj(OFF_DQ, DA_QK)
    q = _rope(dq * _group_rms_scale(dq, ones_bd, DA_HEAD_DIM) * gq_ref[...], rc, ru, rd)
    q_ref[0] = (q * (DA_HEAD_DIM ** -0.5 * LOG2E)).astype(BF16)
    dk = proj(OFF_DK, DA_QK)
    k = _rope(dk * _group_rms_scale(dk, ones_bd, DA_HEAD_DIM) * gk_ref[...], rc, ru, rd)
    k_ref[0] = k
    dv = proj(OFF_DV, DA_WIDTH)
    for h in range(DA_HEADS):
        v_ref[0, pl.ds(h, tm, stride=DA_HEADS), :] = dv[:, h * DA_VDIM:(h + 1) * DA_VDIM]
    if emit_attn_layouts:
        kb_ref[0] = k.astype(BF16)
        for h in range(DA_HEADS):
            vt_ref[0, h, 0, :DA_VDIM, :] = dv[:, h * DA_VDIM:(h + 1) * DA_VDIM].T.astype(BF16)
            vt_ref[0, h, 0, DA_VDIM:, :] = jnp.ones((VT_ONES, tm), BF16)
    gd_ref[0] = _silu(proj(OFF_DG, DA_WIDTH)).astype(BF16)

    lx = proj(OFF_LX, LRU_WIDTH)
    xbuf[pl.ds(CONV_PAD, tm), :] = lx
    xc = cb_ref[...]
    for j in range(CONV_W):
        xc = xc + xbuf[pl.ds(CONV_PAD - (CONV_W - 1) + j, tm), :] * cw_ref[j:j + 1, :]
    tail = xbuf[pl.ds(CONV_PAD + tm - (CONV_W - 1), CONV_W - 1), :]
    cout_ref[0] = tail
    xbuf[pl.ds(CONV_PAD - (CONV_W - 1), CONV_W - 1), :] = tail
    xcb = xc.astype(BF16)
    r = _sigmoid(_dot(xcb, wa_ref[...]) + ba_ref[...])
    i = _sigmoid(_dot(xcb, wx_ref[...]) + bx_ref[...])
    neg_lam = -lam_ref[...]
    softplus = jnp.maximum(neg_lam, 0.0) + jnp.log1p(jnp.exp(-jnp.abs(neg_lam)))
    log_a = (-LRU_C) * r * softplus
    a = jnp.exp(log_a)
    bb = (1.0 + a) * jnp.sqrt(-jnp.tanh(0.5 * log_a)) * (i * xc)
    row = lax.broadcasted_iota(jnp.int32, a.shape, 0)
    bb = bb + jnp.where(row == 0, a * hc[...], 0.0)
    hseq = _lru_scan(a, bb)
    h_last = hseq[tm - 1:tm, :]
    hc[...] = h_last
    hout_ref[0] = h_last
    out_b = hseq * _silu(proj(OFF_LG, LRU_WIDTH))

    mq = proj(OFF_MQ, MX_WIDTH)
    qm = (mq * _group_rms_scale(mq, ones_bd, MX_HEAD_DIM) * gmq_ref[...]
          * (MX_HEAD_DIM ** -0.5)).astype(BF16)
    mk = mk_ref[0].astype(BF16)
    mv = mv_ref[0].astype(BF16)
    lane_q = lax.broadcasted_iota(jnp.int32, (tm, LANES), 1)
    lane_v = lax.broadcasted_iota(jnp.int32, (N_MEM, LANES), 1)
    slabs = []
    for sl in range(MX_WIDTH // LANES):
        qs = qm[:, sl * LANES:(sl + 1) * LANES]
        ks = mk[:, sl * LANES:(sl + 1) * LANES]
        vs = mv[:, sl * LANES:(sl + 1) * LANES]
        acc = None
        rl = None
        for half in range(2):
            sel_q = (lane_q >= MX_HEAD_DIM) if half else (lane_q < MX_HEAD_DIM)
            sel_v = (lane_v >= MX_HEAD_DIM) if half else (lane_v < MX_HEAD_DIM)
            s = _dot_nt(jnp.where(sel_q, qs, jnp.zeros_like(qs)), ks)
            p = jnp.exp(s - jnp.max(s, axis=-1, keepdims=True))
            rsum = 1.0 / jnp.sum(p, axis=-1, keepdims=True)
            o = _dot(p.astype(BF16), jnp.where(sel_v, vs, jnp.zeros_like(vs)))
            acc = o if acc is None else acc + o
            rl = rsum if rl is None else jnp.where(sel_q, rsum, rl)
        slabs.append(acc * rl)
    om = jnp.concatenate(slabs, axis=1)
    out_c = om * _silu(proj(OFF_MG, MX_WIDTH))
    mix_ref[0] = jnp.concatenate([out_b, out_c], axis=1).astype(BF16)


def _in_proj(x, cbuf, h0, mk, mv, wts, rope_tabs, tm, emit_attn_layouts):
    b, t, _ = x.shape
    nt = t // tm
    full = lambda shape: pl.BlockSpec(shape, lambda i, j: (0,) * len(shape))
    per_b = lambda shape: pl.BlockSpec(shape, lambda i, j: (i,) + (0,) * (len(shape) - 1))
    tile = lambda width: pl.BlockSpec((1, tm, width), lambda i, j: (i, j, 0))
    rope_spec = pl.BlockSpec((tm, LANES), lambda i, j: (j, 0))
    in_specs = [
        tile(D_MODEL), per_b((1, CONV_W - 1, LRU_WIDTH)), per_b((1, 1, LRU_WIDTH)),
        full((D_MODEL, IN_WIDTH)), full((1, D_MODEL)), full((1, DA_QK)), full((1, DA_QK)),
        full((1, MX_WIDTH)), full((MXU_TILE, MXU_TILE)),
        rope_spec, rope_spec, rope_spec,
        full((CONV_W, LRU_WIDTH)), full((1, LRU_WIDTH)),
        full((LRU_WIDTH, LRU_WIDTH)), full((1, LRU_WIDTH)),
        full((LRU_WIDTH, LRU_WIDTH)), full((1, LRU_WIDTH)), full((1, LRU_WIDTH)),
        per_b((1, N_MEM, MX_WIDTH)), per_b((1, N_MEM, MX_WIDTH)),
    ]
    out_specs = [tile(DA_QK), tile(DA_QK),
                 pl.BlockSpec((1, tm * DA_HEADS, DA_VDIM), lambda i, j: (i, j, 0)),
                 tile(DA_WIDTH), tile(DA_WIDTH),
                 per_b((1, CONV_W - 1, LRU_WIDTH)), per_b((1, 1, LRU_WIDTH))]
    out_shape = [jax.ShapeDtypeStruct((b, t, DA_QK), BF16),
                 jax.ShapeDtypeStruct((b, t, DA_QK), F32),
                 jax.ShapeDtypeStruct((b, t * DA_HEADS, DA_VDIM), F32),
                 jax.ShapeDtypeStruct((b, t, DA_WIDTH), BF16),
                 jax.ShapeDtypeStruct((b, t, DA_WIDTH), BF16),
                 jax.ShapeDtypeStruct((b, CONV_W - 1, LRU_WIDTH), F32),
                 jax.ShapeDtypeStruct((b, 1, LRU_WIDTH), F32)]
    if emit_attn_layouts:
        out_specs += [tile(DA_QK),
                      pl.BlockSpec((1, DA_HEADS, 1, DA_VDIM + VT_ONES, tm), lambda i, j: (i, 0, j, 0, 0))]
        out_shape += [jax.ShapeDtypeStruct((b, t, DA_QK), BF16),
                      jax.ShapeDtypeStruct((b, DA_HEADS, nt, DA_VDIM + VT_ONES, tm), BF16)]
    return pl.pallas_call(
        functools.partial(_in_proj_kernel, emit_attn_layouts, tm),
        grid=(b, nt),
        in_specs=in_specs,
        out_specs=out_specs,
        out_shape=out_shape,
        scratch_shapes=[pltpu.VMEM((CONV_PAD + tm, LRU_WIDTH), F32), pltpu.VMEM((1, LRU_WIDTH), F32)],
        compiler_params=pltpu.CompilerParams(
            dimension_semantics=("arbitrary", "arbitrary"), vmem_limit_bytes=VMEM_LIMIT_BYTES),
        name="in_proj_prompt" if emit_attn_layouts else "in_proj_sample",
    )(x, cbuf, h0, wts["w_in"], wts["norm_g"], wts["gq"], wts["gk"], wts["gmq"], wts["ones_bd"],
      *rope_tabs, wts["conv_w"], wts["conv_b"], wts["wa"], wts["ba"], wts["wx"], wts["bx"],
      wts["lru_lambda"], mk, mv)


def _lambda_value(lam_ref):
    lp = lam_ref[...]
    s1 = jnp.sum(lp[0:1] * lp[1:2], axis=-1, keepdims=True)
    s2 = jnp.sum(lp[2:3] * lp[3:4], axis=-1, keepdims=True)
    return jnp.exp(s1) - jnp.exp(s2) + LAMBDA_INIT


def _attn_prompt_kernel(tq, tk, hb, q_ref, qn_ref, k_ref, feat_ref, vt_ref, gd_ref, sg_ref, lam_ref,
                        x_ref, mix_ref, wo_ref, y_ref, w_scr, s_scr, tmax_scr, m_scr, acc_scr):
    qi = pl.program_id(2)
    cw = 2 * tq
    lane = lax.broadcasted_iota(jnp.int32, (tq, LANES), 1)
    row = lax.broadcasted_iota(jnp.int32, (tq, LANES), 0)

    def build_w(slot, src_ref, q_tile):
        qfeat = jnp.where(lane > ((q_tile * tq + row) >> 6), NEG_BIG, 0.0).astype(BF16)
        for h in range(hb):
            q = src_ref[0, :, h * LANES:(h + 1) * LANES]
            zero = jnp.zeros_like(q)
            w_scr[slot, h] = jnp.concatenate(
                [jnp.concatenate([jnp.where(lane < DA_HEAD_DIM, q, zero), qfeat], axis=1),
                 jnp.concatenate([jnp.where(lane >= DA_HEAD_DIM, q, zero), qfeat], axis=1)], axis=0)

    cur = qi % 2
    nxt = 1 - cur
    n = ((qi + 1) * tq + tk - 1) // tk

    def key_rows(j, h):
        off = pl.multiple_of(j * tk, tk)
        return jnp.concatenate([k_ref[0, pl.ds(off, tk), h * LANES:(h + 1) * LANES],
                                feat_ref[pl.ds(off, tk), :]], axis=1)

    def stage_scores(kx, slot, h, cc):
        cols = slice(h * cw + cc * MXU_TILE, h * cw + (cc + 1) * MXU_TILE)
        s = _dot_nt(kx, w_scr[slot, h, cc * MXU_TILE:(cc + 1) * MXU_TILE, :])
        s_scr[:, cols] = s
        tmax_scr[:, cols] = jnp.max(s, axis=0, keepdims=True)

    @pl.when(qi == 0)
    def _():
        build_w(0, q_ref, qi)
        for h in range(hb):
            kx = key_rows(0, h)
            for cc in range(cw // MXU_TILE):
                stage_scores(kx, 0, h, cc)

    build_w(nxt, qn_ref, qi + 1)

    m_scr[...] = jnp.full(m_scr.shape, NEG_BIG, F32)
    acc_scr[...] = jnp.zeros(acc_scr.shape, F32)

    def body(j, c):
        m_old = m_scr[...]
        m_new = jnp.maximum(m_old, tmax_scr[...])
        alpha = jnp.exp2(m_old - m_new)
        m_scr[...] = m_new
        last = j + 1 >= n
        j_next = jnp.where(last, 0, j + 1)
        slot = jnp.where(last, nxt, cur)
        for h in range(hb):
            kx = key_rows(j_next, h)
            for cc in range(cw // MXU_TILE):
                cols = slice(h * cw + cc * MXU_TILE, h * cw + (cc + 1) * MXU_TILE)
                acc_c = alpha[:, cols] * acc_scr[:, cols]
                for r in range(tk // MXU_TILE):
                    rows = slice(r * MXU_TILE, (r + 1) * MXU_TILE)
                    p = jnp.exp2(s_scr[rows, cols] - m_new[:, cols]).astype(BF16)
                    acc_c = acc_c + _dot(vt_ref[0, h, j, :, rows], p)
                acc_scr[:, cols] = acc_c
                stage_scores(kx, slot, h, cc)
        return c

    lax.fori_loop(0, n, body, 0)

    acc = acc_scr[...]
    acc = acc[:DA_VDIM] * (1.0 / acc[DA_VDIM:DA_VDIM + 1])
    lam = _lambda_value(lam_ref)
    heads = []
    for h in range(hb):
        ot = acc[:, h * cw:h * cw + tq] - lam * acc[:, h * cw + tq:(h + 1) * cw]
        ot = ot * lax.rsqrt(jnp.mean(ot * ot, axis=0, keepdims=True) + EPS)
        o = ot.T * (sg_ref[...] * (1.0 - LAMBDA_INIT))
        heads.append((o * gd_ref[0, :, h * LANES:(h + 1) * LANES].astype(F32)).astype(BF16))
    out_a = jnp.concatenate(heads, axis=1)
    y_ref[0] = x_ref[0] + _dot(out_a, wo_ref[:DA_WIDTH, :]) + _dot(mix_ref[0], wo_ref[DA_WIDTH:, :])


def _attn_prompt(q, kb, vt, gd, subln_g, lam_params, x, mix, w_out, tq, tk, hb):
    b, t, _ = q.shape
    assert hb == DA_HEADS, "the fused output projection needs every head of a query tile in one step"
    nk, nq = t // tk, t // tq
    assert t % CHUNK == 0 and t // CHUNK <= LANES, "chunk one-hot must fit one 128-lane slab"
    assert DA_HEADS % hb == 0
    key_chunk = np.arange(t) // CHUNK
    feat = jnp.asarray(key_chunk[:, None] == np.arange(LANES)[None, :], BF16)
    hw = hb * LANES
    vrows = DA_VDIM + VT_ONES
    once = dict(pipeline_mode=pl.Buffered(1))
    return pl.pallas_call(
        functools.partial(_attn_prompt_kernel, tq, tk, hb),
        grid=(b, DA_HEADS // hb, nq),
        in_specs=[
            pl.BlockSpec((1, tq, hw), lambda i, h, j: (i, j, h)),
            pl.BlockSpec((1, tq, hw), lambda i, h, j: (i, jnp.minimum(j + 1, nq - 1), h)),
            pl.BlockSpec((1, t, hw), lambda i, h, j: (i, 0, h), **once),
            pl.BlockSpec((t, LANES), lambda i, h, j: (0, 0), **once),
            pl.BlockSpec((1, hb, nk, vrows, tk), lambda i, h, j: (i, h, 0, 0, 0), **once),
            pl.BlockSpec((1, tq, hw), lambda i, h, j: (i, j, h)),
            pl.BlockSpec((1, DA_VDIM), lambda i, h, j: (0, 0)),
            pl.BlockSpec((4, DA_HEAD_DIM), lambda i, h, j: (0, 0)),
            pl.BlockSpec((1, tq, D_MODEL), lambda i, h, j: (i, j, 0)),
            pl.BlockSpec((1, tq, DA_WIDTH), lambda i, h, j: (i, j, 0)),
            pl.BlockSpec((D_MODEL, D_MODEL), lambda i, h, j: (0, 0), **once),
        ],
        out_specs=pl.BlockSpec((1, tq, D_MODEL), lambda i, h, j: (i, j, 0)),
        out_shape=jax.ShapeDtypeStruct((b, t, D_MODEL), F32),
        scratch_shapes=[pltpu.VMEM((2, hb, 2 * tq, 2 * LANES), BF16),
                        pltpu.VMEM((tk, hb * 2 * tq), F32),
                        pltpu.VMEM((1, hb * 2 * tq), F32), pltpu.VMEM((1, hb * 2 * tq), F32),
                        pltpu.VMEM((vrows, hb * 2 * tq), F32)],
        compiler_params=pltpu.CompilerParams(
            dimension_semantics=("arbitrary", "arbitrary", "arbitrary"),
            vmem_limit_bytes=VMEM_LIMIT_BYTES),
        name="diff_attn_prompt",
    )(q, q, kb, feat, vt, gd, subln_g, lam_params, x, mix, w_out)


def _attn_sample_kernel(past, q_ref, kn_ref, vn_ref, kp_ref, vp_ref, gd_ref, sg_ref, lam_ref, o_ref):
    ts = q_ref.shape[1]
    lam = _lambda_value(lam_ref)
    lane = lax.broadcasted_iota(jnp.int32, (ts, LANES), 1)
    qrow = past + lax.broadcasted_iota(jnp.int32, (ts, past), 0)
    vis_p = (lax.broadcasted_iota(jnp.int32, (ts, past), 1) >> 6) <= (qrow >> 6)
    qrow_n = past + lax.broadcasted_iota(jnp.int32, (ts, ts), 0)
    vis_n = ((past + lax.broadcasted_iota(jnp.int32, (ts, ts), 1)) >> 6) <= (qrow_n >> 6)
    for h in range(DA_HEADS):
        sl = slice(h * LANES, (h + 1) * LANES)
        q = q_ref[0, :, sl]
        kp_t = kp_ref[0, sl, :].astype(BF16)
        kn = kn_ref[0, :, sl].astype(BF16)
        zero = jnp.zeros_like(q)
        wp, wn = [], []
        for c in range(2):
            qc = jnp.where((lane >= DA_HEAD_DIM) if c else (lane < DA_HEAD_DIM), q, zero)
            sp = jnp.where(vis_p, _dot(qc, kp_t), NEG_BIG)
            sn = jnp.where(vis_n, _dot_nt(qc, kn), NEG_BIG)
            m = jnp.maximum(jnp.max(sp, axis=-1, keepdims=True), jnp.max(sn, axis=-1, keepdims=True))
            pp = jnp.exp2(sp - m)
            pn = jnp.exp2(sn - m)
            inv = 1.0 / (jnp.sum(pp, axis=-1, keepdims=True) + jnp.sum(pn, axis=-1, keepdims=True))
            wp.append(pp * inv)
            wn.append(pn * inv)
        wgt_p = (wp[0] - lam * wp[1]).astype(BF16)
        wgt_n = (wn[0] - lam * wn[1]).astype(BF16)
        v_past = vp_ref[0, pl.ds(h, past, stride=DA_HEADS), :].astype(BF16)
        v_new = vn_ref[0, pl.ds(h, ts, stride=DA_HEADS), :].astype(BF16)
        o = _dot(wgt_p, v_past) + _dot(wgt_n, v_new)
        o = o * lax.rsqrt(jnp.mean(o * o, axis=-1, keepdims=True) + EPS)
        o = o * (sg_ref[...] * (1.0 - LAMBDA_INIT))
        o_ref[0, :, sl] = (o * gd_ref[0, :, sl].astype(F32)).astype(BF16)


def _attn_sample(q, k_new, v_new, k_past_t, v_past, gd, subln_g, lam_params):
    b, ts, _ = q.shape
    past = k_past_t.shape[2]
    new = pl.BlockSpec((1, ts, DA_WIDTH), lambda i: (i, 0, 0))
    new_v = pl.BlockSpec((1, ts * DA_HEADS, DA_VDIM), lambda i: (i, 0, 0))
    old_kt = pl.BlockSpec((1, DA_QK, past), lambda i: (i, 0, 0))
    old_v = pl.BlockSpec((1, past * DA_HEADS, DA_VDIM), lambda i: (i, 0, 0))
    return pl.pallas_call(
        functools.partial(_attn_sample_kernel, past),
        grid=(b,),
        in_specs=[new, new, new_v, old_kt, old_v, new,
                  pl.BlockSpec((1, DA_VDIM), lambda i: (0, 0)),
                  pl.BlockSpec((4, DA_HEAD_DIM), lambda i: (0, 0))],
        out_specs=new,
        out_shape=jax.ShapeDtypeStruct((b, ts, DA_WIDTH), BF16),
        compiler_params=pltpu.CompilerParams(
            dimension_semantics=("arbitrary",), vmem_limit_bytes=VMEM_LIMIT_BYTES),
        name="diff_attn_sample",
    )(q, k_new, v_new, k_past_t, v_past, gd, subln_g, lam_params)


def _out_proj_kernel(x_ref, a_ref, mix_ref, w_ref, y_ref):
    y_ref[0] = (x_ref[0] + _dot(a_ref[0], w_ref[:DA_WIDTH, :]) + _dot(mix_ref[0], w_ref[DA_WIDTH:, :]))


def _out_proj(x, out_a, mix, w_out, tm):
    b, t, _ = x.shape
    tile = lambda width: pl.BlockSpec((1, tm, width), lambda i, j: (i, j, 0))
    return pl.pallas_call(
        _out_proj_kernel,
        grid=(b, t // tm),
        in_specs=[tile(D_MODEL), tile(DA_WIDTH), tile(DA_WIDTH),
                  pl.BlockSpec((D_MODEL, D_MODEL), lambda i, j: (0, 0))],
        out_specs=tile(D_MODEL),
        out_shape=jax.ShapeDtypeStruct((b, t, D_MODEL), F32),
        compiler_params=pltpu.CompilerParams(
            dimension_semantics=("arbitrary", "arbitrary"), vmem_limit_bytes=VMEM_LIMIT_BYTES),
        name="out_proj",
    )(x, out_a, mix, w_out)


def _rope_tables(pos):
    half = ROT_DIM // 2
    inv = ROPE_THETA ** (-jnp.arange(0, ROT_DIM, 2, dtype=F32) / ROT_DIM)
    ang = pos.astype(F32)[:, None] * inv[None, :]
    cos, sin = jnp.cos(ang), jnp.sin(ang)
    t = pos.shape[0]
    ones = jnp.ones((t, DA_HEAD_DIM - ROT_DIM), F32)
    zeros = jnp.zeros((t, DA_HEAD_DIM - ROT_DIM), F32)
    zh = jnp.zeros((t, half), F32)
    c = jnp.concatenate([cos, cos, ones], axis=1)
    s_up = jnp.concatenate([-sin, zh, zeros], axis=1)
    s_dn = jnp.concatenate([zh, sin, zeros], axis=1)
    rep = LANES // DA_HEAD_DIM
    return tuple(jnp.tile(a, (1, rep)) for a in (c, s_up, s_dn))


def _block_diag(w):
    n, bi, bj = w.shape
    eye = jnp.eye(n, dtype=w.dtype)
    return (eye[:, None, :, None] * w[:, :, None, :]).reshape(n * bi, n * bj)


def _pick_tile(t, pref):
    return pref if t % pref == 0 else t


def kernel(x_prompt, x_sample, mem_prompt, cache_diff_k, cache_diff_v, cache_mem_k, cache_mem_v,
           state_lru_conv, state_lru_h, norm_g, w_in, da_q_norm_g, da_k_norm_g, lambda_q1, lambda_k1,
           lambda_q2, lambda_k2, da_subln_g, lru_conv_w, lru_conv_b, lru_w_a, lru_b_a, lru_w_x, lru_b_x,
           lru_lambda, mem_norm_g, w_mem_kv, mx_q_norm_g, mx_k_norm_g, w_out):
    depth = w_in.shape[0]
    assert depth == 1
    bp, tp, _ = x_prompt.shape
    bs, ts, _ = x_sample.shape
    past = cache_diff_k.shape[2]
    l = 0
    grp = np.arange(MXU_TILE) // DA_HEAD_DIM
    ones_bd = jnp.asarray(grp[:, None] == grp[None, :], BF16)
    wts = dict(
        w_in=w_in[l].astype(BF16), norm_g=norm_g[l][None, :],
        gq=jnp.tile(da_q_norm_g[l], DA_QK // DA_HEAD_DIM)[None, :],
        gk=jnp.tile(da_k_norm_g[l], DA_QK // DA_HEAD_DIM)[None, :],
        gmq=jnp.tile(mx_q_norm_g[l], MX_HEADS)[None, :], ones_bd=ones_bd,
        conv_w=lru_conv_w[l], conv_b=lru_conv_b[l][None, :],
        wa=_block_diag(lru_w_a[l]).astype(BF16), ba=lru_b_a[l][None, :],
        wx=_block_diag(lru_w_x[l]).astype(BF16), bx=lru_b_x[l][None, :],
        lru_lambda=lru_lambda[l][None, :])
    lam_params = jnp.stack([lambda_q1[l], lambda_k1[l], lambda_q2[l], lambda_k2[l]])
    subln_g = da_subln_g[l][None, :]
    w_out_b = w_out[l].astype(BF16)

    mk_p, mv_p = _mem_kv(mem_prompt, mem_norm_g[l][None, :], w_mem_kv[l].astype(BF16),
                         jnp.tile(mx_k_norm_g[l], MX_HEADS)[None, :], ones_bd)
    tm_p = _pick_tile(tp, 512)
    tq_p = _pick_tile(tp, 512)
    pos_p = jnp.arange(tp, dtype=jnp.int32)
    q_p, k_p, v_p, gd_p, mix_p, c_p, h_p, kb_p, vt_p = _in_proj(
        x_prompt, jnp.zeros((bp, CONV_W - 1, LRU_WIDTH), F32), jnp.zeros((bp, 1, LRU_WIDTH), F32),
        mk_p, mv_p, wts, _rope_tables(pos_p), tm_p, True)
    y_p = _attn_prompt(q_p, kb_p, vt_p, gd_p, subln_g, lam_params, x_prompt, mix_p, w_out_b,
                       tq_p, tm_p, ATTN_HEADS_PER_STEP)

    pos_s = past + jnp.arange(ts, dtype=jnp.int32)
    q_s, k_s, v_s, gd_s, mix_s, c_s, h_s = _in_proj(
        x_sample, state_lru_conv[l], state_lru_h[l][:, None, :],
        cache_mem_k[l].reshape(bs, N_MEM, MX_WIDTH), cache_mem_v[l].reshape(bs, N_MEM, MX_WIDTH),
        wts, _rope_tables(pos_s), ts, False)
    oa_s = _attn_sample(q_s, k_s, v_s, jnp.swapaxes(cache_diff_k[l].reshape(bs, past, DA_QK), 1, 2),
                        cache_diff_v[l].reshape(bs, past * DA_HEADS, DA_VDIM), gd_s, subln_g, lam_params)
    y_s = _out_proj(x_sample, oa_s, mix_s, w_out_b, ts)

    return (y_p, y_s,
            k_p.reshape(1, bp, tp, DA_HEADS, 2, DA_HEAD_DIM), v_p.reshape(1, bp, tp, DA_HEADS, DA_VDIM),
            mk_p.reshape(1, bp, N_MEM, MX_HEADS, MX_HEAD_DIM), mv_p.reshape(1, bp, N_MEM, MX_HEADS, MX_HEAD_DIM),
            c_p[None], h_p.reshape(1, bp, LRU_WIDTH),
            k_s.reshape(1, bs, ts, DA_HEADS, 2, DA_HEAD_DIM), v_s.reshape(1, bs, ts, DA_HEADS, DA_VDIM),
            c_s[None], h_s.reshape(1, bs, LRU_WIDTH))
```

```python
import functools
import math

import numpy as np
import jax
import jax.numpy as jnp
from jax import lax
from jax.experimental import pallas as pl
from jax.experimental.pallas import tpu as pltpu

D_MODEL = 1024
CHUNK = 64
EPS = 1e-6
DA_HEADS = 4
DA_HEAD_DIM = 64
DA_VDIM = 128
DA_QK = 512
DA_WIDTH = 512
ROPE_THETA = 500000.0
ROT_DIM = 16
LRU_WIDTH = 256
LRU_BLOCKS = 4
CONV_W = 4
LRU_C = 8.0
MX_HEADS = 4
MX_HEAD_DIM = 64
MX_WIDTH = 256
N_MEM = 256
LAMBDA_INIT = 0.8 - 0.6 * math.exp(-0.3 * 0)
NEG_BIG = -1e30
LOG2E = math.log2(math.e)

OFF_DQ, OFF_DK, OFF_DV, OFF_DG = 0, 512, 1024, 1536
OFF_LX, OFF_LG, OFF_MQ, OFF_MG = 2048, 2304, 2560, 2816
IN_WIDTH = 3072

VMEM_LIMIT_BYTES = 52 * 1024 * 1024
LANES = 128
MXU_TILE = 256
CONV_PAD = 8
ATTN_HEADS_PER_STEP = 4
SCAN_CHUNK = 32
VT_ONES = 16

F32 = jnp.float32
BF16 = jnp.bfloat16


def _dot(a, b):
    return jnp.dot(a, b, preferred_element_type=F32)


def _dot_nt(a, b):
    return lax.dot_general(a, b, (((1,), (1,)), ((), ())), preferred_element_type=F32)


def _sigmoid(x):
    return 0.5 + 0.5 * jnp.tanh(0.5 * x)


def _silu(x):
    return x * _sigmoid(x)


def _group_rms_scale(x, ones_bd, group):
    xsq = (x * x).astype(BF16)
    wb = ones_bd.shape[0]
    ss = jnp.concatenate([_dot(xsq[:, c:c + wb], ones_bd) for c in range(0, x.shape[1], wb)], axis=1)
    return lax.rsqrt(ss * (1.0 / group) + EPS)


def _rope(x, c, s_up, s_dn):
    w = x.shape[1]
    return x * c + pltpu.roll(x, w - ROT_DIM // 2, 1) * s_up + pltpu.roll(x, ROT_DIM // 2, 1) * s_dn


def _mem_kv_kernel(mem_ref, g_ref, w_ref, gk_ref, ones_ref, k_ref, v_ref):
    x = mem_ref[0]
    y = x * lax.rsqrt(jnp.mean(x * x, axis=-1, keepdims=True) + EPS) * g_ref[...]
    kv = _dot(y.astype(BF16), w_ref[...])
    k = kv[:, :MX_WIDTH]
    k_ref[0] = k * _group_rms_scale(k, ones_ref[...], MX_HEAD_DIM) * gk_ref[...]
    v_ref[0] = kv[:, MX_WIDTH:]


def _mem_kv(mem, mem_norm_g, w_mem_kv, gk_tiled, ones256):
    b, n, _ = mem.shape
    full = lambda shape: pl.BlockSpec(shape, lambda i: (0,) * len(shape))
    return pl.pallas_call(
        _mem_kv_kernel,
        grid=(b,),
        in_specs=[
            pl.BlockSpec((1, n, D_MODEL), lambda i: (i, 0, 0)),
            full((1, D_MODEL)),
            full((D_MODEL, 2 * MX_WIDTH)),
            full((1, MX_WIDTH)),
            full((MX_WIDTH, MX_WIDTH)),
        ],
        out_specs=[pl.BlockSpec((1, n, MX_WIDTH), lambda i: (i, 0, 0))] * 2,
        out_shape=[jax.ShapeDtypeStruct((b, n, MX_WIDTH), F32)] * 2,
        compiler_params=pltpu.CompilerParams(
            dimension_semantics=("arbitrary",), vmem_limit_bytes=VMEM_LIMIT_BYTES),
        name="mem_kv",
    )(mem, mem_norm_g, w_mem_kv, gk_tiled, ones256)


def _lru_scan(a, b, h_prev):
    tm = a.shape[0]
    chunk = min(tm, SCAN_CHUNK)
    row = lax.broadcasted_iota(jnp.int32, (chunk, a.shape[1]), 0)
    out = []
    for c0 in range(0, tm, chunk):
        ac, bc = a[c0:c0 + chunk], b[c0:c0 + chunk]
        d = 1
        while d < chunk:
            keep = row >= d
            bc = ac * jnp.where(keep, pltpu.roll(bc, d, 0), 0.0) + bc
            ac = ac * jnp.where(keep, pltpu.roll(ac, d, 0), 1.0)
            d *= 2
        hc = bc + ac * h_prev
        h_prev = hc[chunk - 1:chunk, :]
        out.append(hc)
    return jnp.concatenate(out, axis=0)


def _in_proj_kernel(emit_attn_layouts, tm,
                    x_ref, cbuf_ref, h0_ref, w_ref, ng_ref, gq_ref, gk_ref, gmq_ref, ones_ref,
                    rc_ref, ru_ref, rd_ref, cw_ref, cb_ref, wa_ref, ba_ref, wx_ref, bx_ref,
                    lam_ref, mk_ref, mv_ref, *rest):
    if emit_attn_layouts:
        (q_ref, k_ref, v_ref, gd_ref, mix_ref, cout_ref, hout_ref, kb_ref, vt_ref,
         xbuf, hc) = rest
    else:
        q_ref, k_ref, v_ref, gd_ref, mix_ref, cout_ref, hout_ref, xbuf, hc = rest
    t = pl.program_id(1)

    @pl.when(t == 0)
    def _():
        xbuf[pl.ds(CONV_PAD - (CONV_W - 1), CONV_W - 1), :] = cbuf_ref[0]
        hc[...] = h0_ref[0]

    x = x_ref[0]
    hn = (x * lax.rsqrt(jnp.mean(x * x, axis=-1, keepdims=True) + EPS) * ng_ref[...]).astype(BF16)

    def proj(off, width):
        return _dot(hn, w_ref[:, off:off + width])

    rc = jnp.concatenate([rc_ref[...]] * (DA_QK // LANES), axis=1)
    ru = jnp.concatenate([ru_ref[...]] * (DA_QK // LANES), axis=1)
    rd = jnp.concatenate([rd_ref[...]] * (DA_QK // LANES), axis=1)
    ones_bd = ones_ref[...]

    dq = proj(OFF_DQ, DA_QK)
    q = _rope(dq * _group_rms_scale(dq, ones_bd, DA_HEAD_DIM) * gq_ref[...], rc, ru, rd)
    q_ref[0] = (q * (DA_HEAD_DIM ** -0.5 * LOG2E)).astype(BF16)
    dk = proj(OFF_DK, DA_QK)
    k = _rope(dk * _group_rms_scale(dk, ones_bd, DA_HEAD_DIM) * gk_ref[...], rc, ru, rd)
    k_ref[0] = k
    dv = proj(OFF_DV, DA_WIDTH)
    for h in range(DA_HEADS):
        v_ref[0, pl.ds(h, tm, stride=DA_HEADS), :] = dv[:, h * DA_VDIM:(h + 1) * DA_VDIM]
    if emit_attn_layouts:
        kb_ref[0] = k.astype(BF16)
        for h in range(DA_HEADS):
            vt_ref[0, h, 0, :DA_VDIM, :] = dv[:, h * DA_VDIM:(h + 1) * DA_VDIM].T.astype(BF16)
            vt_ref[0, h, 0, DA_VDIM:, :] = jnp.ones((VT_ONES, tm), BF16)
    gd_ref[0] = _silu(proj(OFF_DG, DA_WIDTH)).astype(BF16)

    lx = proj(OFF_LX, LRU_WIDTH)
    xbuf[pl.ds(CONV_PAD, tm), :] = lx
    xc = cb_ref[...]
    for j in range(CONV_W):
        xc = xc + xbuf[pl.ds(CONV_PAD - (CONV_W - 1) + j, tm), :] * cw_ref[j:j + 1, :]
    tail = xbuf[pl.ds(CONV_PAD + tm - (CONV_W - 1), CONV_W - 1), :]
    cout_ref[0] = tail
    xbuf[pl.ds(CONV_PAD - (CONV_W - 1), CONV_W - 1), :] = tail
    xcb = xc.astype(BF16)
    r = _sigmoid(_dot(xcb, wa_ref[...]) + ba_ref[...])
    i = _sigmoid(_dot(xcb, wx_ref[...]) + bx_ref[...])
    neg_lam = -lam_ref[...]
    softplus = jnp.maximum(neg_lam, 0.0) + jnp.log1p(jnp.exp(-jnp.abs(neg_lam)))
    log_a = (-LRU_C) * r * softplus
    a = jnp.exp(log_a)
    bb = (1.0 + a) * jnp.sqrt(-jnp.tanh(0.5 * log_a)) * (i * xc)
    hseq = _lru_scan(a, bb, hc[...])
    h_last = hseq[tm - 1:tm, :]
    hc[...] = h_last
    hout_ref[0] = h_last
    out_b = hseq * _silu(proj(OFF_LG, LRU_WIDTH))

    mq = proj(OFF_MQ, MX_WIDTH)
    qm = (mq * _group_rms_scale(mq, ones_bd, MX_HEAD_DIM) * gmq_ref[...]
          * (MX_HEAD_DIM ** -0.5 * LOG2E)).astype(BF16)
    mk = mk_ref[0].astype(BF16)
    mv = mv_ref[0].astype(BF16)
    lane_q = lax.broadcasted_iota(jnp.int32, (tm, LANES), 1)
    lane_v = lax.broadcasted_iota(jnp.int32, (N_MEM, LANES), 1)
    slabs = []
    for sl in range(MX_WIDTH // LANES):
        qs = qm[:, sl * LANES:(sl + 1) * LANES]
        ks = mk[:, sl * LANES:(sl + 1) * LANES]
        vs = mv[:, sl * LANES:(sl + 1) * LANES]
        acc = None
        rl = None
        for half in range(2):
            sel_q = (lane_q >= MX_HEAD_DIM) if half else (lane_q < MX_HEAD_DIM)
            sel_v = (lane_v >= MX_HEAD_DIM) if half else (lane_v < MX_HEAD_DIM)
            s = _dot_nt(jnp.where(sel_q, qs, jnp.zeros_like(qs)), ks)
            p = jnp.exp2(s - jnp.max(s, axis=-1, keepdims=True))
            rsum = 1.0 / jnp.sum(p, axis=-1, keepdims=True)
            o = _dot(p.astype(BF16), jnp.where(sel_v, vs, jnp.zeros_like(vs)))
            acc = o if acc is None else acc + o
            rl = rsum if rl is None else jnp.where(sel_q, rsum, rl)
        slabs.append(acc * rl)
    om = jnp.concatenate(slabs, axis=1)
    out_c = om * _silu(proj(OFF_MG, MX_WIDTH))
    mix_ref[0] = jnp.concatenate([out_b, out_c], axis=1).astype(BF16)


def _in_proj(x, cbuf, h0, mk, mv, wts, rope_tabs, tm, emit_attn_layouts):
    b, t, _ = x.shape
    nt = t // tm
    full = lambda shape: pl.BlockSpec(shape, lambda i, j: (0,) * len(shape))
    per_b = lambda shape: pl.BlockSpec(shape, lambda i, j: (i,) + (0,) * (len(shape) - 1))
    tile = lambda width: pl.BlockSpec((1, tm, width), lambda i, j: (i, j, 0))
    rope_spec = pl.BlockSpec((tm, LANES), lambda i, j: (j, 0))
    in_specs = [
        tile(D_MODEL), per_b((1, CONV_W - 1, LRU_WIDTH)), per_b((1, 1, LRU_WIDTH)),
        full((D_MODEL, IN_WIDTH)), full((1, D_MODEL)), full((1, DA_QK)), full((1, DA_QK)),
        full((1, MX_WIDTH)), full((MXU_TILE, MXU_TILE)),
        rope_spec, rope_spec, rope_spec,
        full((CONV_W, LRU_WIDTH)), full((1, LRU_WIDTH)),
        full((LRU_WIDTH, LRU_WIDTH)), full((1, LRU_WIDTH)),
        full((LRU_WIDTH, LRU_WIDTH)), full((1, LRU_WIDTH)), full((1, LRU_WIDTH)),
        per_b((1, N_MEM, MX_WIDTH)), per_b((1, N_MEM, MX_WIDTH)),
    ]
    out_specs = [tile(DA_QK), tile(DA_QK),
                 pl.BlockSpec((1, tm * DA_HEADS, DA_VDIM), lambda i, j: (i, j, 0)),
                 tile(DA_WIDTH), tile(DA_WIDTH),
                 per_b((1, CONV_W - 1, LRU_WIDTH)), per_b((1, 1, LRU_WIDTH))]
    out_shape = [jax.ShapeDtypeStruct((b, t, DA_QK), BF16),
                 jax.ShapeDtypeStruct((b, t, DA_QK), F32),
                 jax.ShapeDtypeStruct((b, t * DA_HEADS, DA_VDIM), F32),
                 jax.ShapeDtypeStruct((b, t, DA_WIDTH), BF16),
                 jax.ShapeDtypeStruct((b, t, DA_WIDTH), BF16),
                 jax.ShapeDtypeStruct((b, CONV_W - 1, LRU_WIDTH), F32),
                 jax.ShapeDtypeStruct((b, 1, LRU_WIDTH), F32)]
    if emit_attn_layouts:
        out_specs += [tile(DA_QK),
                      pl.BlockSpec((1, DA_HEADS, 1, DA_VDIM + VT_ONES, tm), lambda i, j: (i, 0, j, 0, 0))]
        out_shape += [jax.ShapeDtypeStruct((b, t, DA_QK), BF16),
                      jax.ShapeDtypeStruct((b, DA_HEADS, nt, DA_VDIM + VT_ONES, tm), BF16)]
    return pl.pallas_call(
        functools.partial(_in_proj_kernel, emit_attn_layouts, tm),
        grid=(b, nt),
        in_specs=in_specs,
        out_specs=out_specs,
        out_shape=out_shape,
        scratch_shapes=[pltpu.VMEM((CONV_PAD + tm, LRU_WIDTH), F32), pltpu.VMEM((1, LRU_WIDTH), F32)],
        compiler_params=pltpu.CompilerParams(
            dimension_semantics=("arbitrary", "arbitrary"), vmem_limit_bytes=VMEM_LIMIT_BYTES),
        name="in_proj_prompt" if emit_attn_layouts else "in_proj_sample",
    )(x, cbuf, h0, wts["w_in"], wts["norm_g"], wts["gq"], wts["gk"], wts["gmq"], wts["ones_bd"],
      *rope_tabs, wts["conv_w"], wts["conv_b"], wts["wa"], wts["ba"], wts["wx"], wts["bx"],
      wts["lru_lambda"], mk, mv)


def _lambda_value(lam_ref):
    lp = lam_ref[...]
    s1 = jnp.sum(lp[0:1] * lp[1:2], axis=-1, keepdims=True)
    s2 = jnp.sum(lp[2:3] * lp[3:4], axis=-1, keepdims=True)
    return jnp.exp(s1) - jnp.exp(s2) + LAMBDA_INIT


def _attn_prompt_kernel(tq, tk, hb, q_ref, qn_ref, k_ref, feat_ref, vt_ref, gd_ref, sg_ref, lam_ref,
                        x_ref, mix_ref, wo_ref, y_ref, w_scr, s_scr, tmax_scr, m_scr, acc_scr):
    qi = pl.program_id(2)
    cw = 2 * tq
    lane = lax.broadcasted_iota(jnp.int32, (tq, LANES), 1)
    row = lax.broadcasted_iota(jnp.int32, (tq, LANES), 0)

    def build_w(slot, src_ref, q_tile):
        qfeat = jnp.where(lane > ((q_tile * tq + row) >> 6), NEG_BIG, 0.0).astype(BF16)
        for h in range(hb):
            q = src_ref[0, :, h * LANES:(h + 1) * LANES]
            zero = jnp.zeros_like(q)
            w_scr[slot, h] = jnp.concatenate(
                [jnp.concatenate([jnp.where(lane < DA_HEAD_DIM, q, zero), qfeat], axis=1),
                 jnp.concatenate([jnp.where(lane >= DA_HEAD_DIM, q, zero), qfeat], axis=1)], axis=0)

    cur = qi % 2
    nxt = 1 - cur
    n = ((qi + 1) * tq + tk - 1) // tk

    def key_rows(j, h):
        off = pl.multiple_of(j * tk, tk)
        return jnp.concatenate([k_ref[0, pl.ds(off, tk), h * LANES:(h + 1) * LANES],
                                feat_ref[pl.ds(off, tk), :]], axis=1)

    def stage_scores(kx, slot, h, cc):
        cols = slice(h * cw + cc * MXU_TILE, h * cw + (cc + 1) * MXU_TILE)
        s = _dot_nt(kx, w_scr[slot, h, cc * MXU_TILE:(cc + 1) * MXU_TILE, :])
        s_scr[:, cols] = s
        tmax_scr[:, cols] = jnp.max(s, axis=0, keepdims=True)

    @pl.when(qi == 0)
    def _():
        build_w(0, q_ref, qi)
        for h in range(hb):
            kx = key_rows(0, h)
            for cc in range(cw // MXU_TILE):
                stage_scores(kx, 0, h, cc)

    build_w(nxt, qn_ref, qi + 1)

    m_scr[...] = jnp.full(m_scr.shape, NEG_BIG, F32)
    acc_scr[...] = jnp.zeros(acc_scr.shape, F32)

    def body(j, c):
        m_old = m_scr[...]
        m_new = jnp.maximum(m_old, tmax_scr[...])
        alpha = jnp.exp2(m_old - m_new)
        m_scr[...] = m_new
        last = j + 1 >= n
        j_next = jnp.where(last, 0, j + 1)
        slot = jnp.where(last, nxt, cur)
        for h in range(hb):
            kx = key_rows(j_next, h)
            for cc in range(cw // MXU_TILE):
                cols = slice(h * cw + cc * MXU_TILE, h * cw + (cc + 1) * MXU_TILE)
                acc_c = alpha[:, cols] * acc_scr[:, cols]
                for r in range(tk // MXU_TILE):
                    rows = slice(r * MXU_TILE, (r + 1) * MXU_TILE)
                    p = jnp.exp2(s_scr[rows, cols] - m_new[:, cols]).astype(BF16)
                    acc_c = acc_c + _dot(vt_ref[0, h, j, :, rows], p)
                acc_scr[:, cols] = acc_c
                stage_scores(kx, slot, h, cc)
        return c

    lax.fori_loop(0, n, body, 0)

    y_ref[0] = x_ref[0] + _dot(mix_ref[0], wo_ref[DA_WIDTH:, :])
    acc = acc_scr[...]
    acc = acc[:DA_VDIM] * (1.0 / acc[DA_VDIM:DA_VDIM + 1])
    lam = _lambda_value(lam_ref)
    heads = []
    for h in range(hb):
        ot = acc[:, h * cw:h * cw + tq] - lam * acc[:, h * cw + tq:(h + 1) * cw]
        ot = ot * lax.rsqrt(jnp.mean(ot * ot, axis=0, keepdims=True) + EPS)
        o = ot.T * (sg_ref[...] * (1.0 - LAMBDA_INIT))
        heads.append((o * gd_ref[0, :, h * LANES:(h + 1) * LANES].astype(F32)).astype(BF16))
    out_a = jnp.concatenate(heads, axis=1)
    y_ref[0] += _dot(out_a, wo_ref[:DA_WIDTH, :])


def _attn_prompt(q, kb, vt, gd, subln_g, lam_params, x, mix, w_out, tq, tk, hb):
    b, t, _ = q.shape
    assert hb == DA_HEADS, "the fused output projection needs every head of a query tile in one step"
    nk, nq = t // tk, t // tq
    assert t % CHUNK == 0 and t // CHUNK <= LANES, "chunk one-hot must fit one 128-lane slab"
    assert DA_HEADS % hb == 0
    key_chunk = np.arange(t) // CHUNK
    feat = jnp.asarray(key_chunk[:, None] == np.arange(LANES)[None, :], BF16)
    hw = hb * LANES
    vrows = DA_VDIM + VT_ONES
    once = dict(pipeline_mode=pl.Buffered(1))
    return pl.pallas_call(
        functools.partial(_attn_prompt_kernel, tq, tk, hb),
        grid=(b, DA_HEADS // hb, nq),
        in_specs=[
            pl.BlockSpec((1, tq, hw), lambda i, h, j: (i, j, h)),
            pl.BlockSpec((1, tq, hw), lambda i, h, j: (i, jnp.minimum(j + 1, nq - 1), h)),
            pl.BlockSpec((1, t, hw), lambda i, h, j: (i, 0, h), **once),
            pl.BlockSpec((t, LANES), lambda i, h, j: (0, 0), **once),
            pl.BlockSpec((1, hb, nk, vrows, tk), lambda i, h, j: (i, h, 0, 0, 0), **once),
            pl.BlockSpec((1, tq, hw), lambda i, h, j: (i, j, h)),
            pl.BlockSpec((1, DA_VDIM), lambda i, h, j: (0, 0)),
            pl.BlockSpec((4, DA_HEAD_DIM), lambda i, h, j: (0, 0)),
            pl.BlockSpec((1, tq, D_MODEL), lambda i, h, j: (i, j, 0)),
            pl.BlockSpec((1, tq, DA_WIDTH), lambda i, h, j: (i, j, 0)),
            pl.BlockSpec((D_MODEL, D_MODEL), lambda i, h, j: (0, 0), **once),
        ],
        out_specs=pl.BlockSpec((1, tq, D_MODEL), lambda i, h, j: (i, j, 0)),
        out_shape=jax.ShapeDtypeStruct((b, t, D_MODEL), F32),
        scratch_shapes=[pltpu.VMEM((2, hb, 2 * tq, 2 * LANES), BF16),
                        pltpu.VMEM((tk, hb * 2 * tq), F32),
                        pltpu.VMEM((1, hb * 2 * tq), F32), pltpu.VMEM((1, hb * 2 * tq), F32),
                        pltpu.VMEM((vrows, hb * 2 * tq), F32)],
        compiler_params=pltpu.CompilerParams(
            dimension_semantics=("arbitrary", "arbitrary", "arbitrary"),
            vmem_limit_bytes=VMEM_LIMIT_BYTES),
        name="diff_attn_prompt",
    )(q, q, kb, feat, vt, gd, subln_g, lam_params, x, mix, w_out)


def _attn_sample_kernel(past, q_ref, kn_ref, vn_ref, kp_ref, vp_ref, gd_ref, sg_ref, lam_ref, o_ref):
    ts = q_ref.shape[1]
    lam = _lambda_value(lam_ref)
    lane = lax.broadcasted_iota(jnp.int32, (ts, LANES), 1)
    qrow = past + lax.broadcasted_iota(jnp.int32, (ts, past), 0)
    vis_p = (lax.broadcasted_iota(jnp.int32, (ts, past), 1) >> 6) <= (qrow >> 6)
    qrow_n = past + lax.broadcasted_iota(jnp.int32, (ts, ts), 0)
    vis_n = ((past + lax.broadcasted_iota(jnp.int32, (ts, ts), 1)) >> 6) <= (qrow_n >> 6)
    for h in range(DA_HEADS):
        sl = slice(h * LANES, (h + 1) * LANES)
        q = q_ref[0, :, sl]
        kp_t = kp_ref[0, sl, :].astype(BF16)
        kn = kn_ref[0, :, sl].astype(BF16)
        zero = jnp.zeros_like(q)
        wp, wn = [], []
        for c in range(2):
            qc = jnp.where((lane >= DA_HEAD_DIM) if c else (lane < DA_HEAD_DIM), q, zero)
            sp = jnp.where(vis_p, _dot(qc, kp_t), NEG_BIG)
            sn = jnp.where(vis_n, _dot_nt(qc, kn), NEG_BIG)
            m = jnp.maximum(jnp.max(sp, axis=-1, keepdims=True), jnp.max(sn, axis=-1, keepdims=True))
            pp = jnp.exp2(sp - m)
            pn = jnp.exp2(sn - m)
            inv = 1.0 / (jnp.sum(pp, axis=-1, keepdims=True) + jnp.sum(pn, axis=-1, keepdims=True))
            wp.append(pp * inv)
            wn.append(pn * inv)
        wgt_p = (wp[0] - lam * wp[1]).astype(BF16)
        wgt_n = (wn[0] - lam * wn[1]).astype(BF16)
        v_past = vp_ref[0, pl.ds(h, past, stride=DA_HEADS), :].astype(BF16)
        v_new = vn_ref[0, pl.ds(h, ts, stride=DA_HEADS), :].astype(BF16)
        o = _dot(wgt_p, v_past) + _dot(wgt_n, v_new)
        o = o * lax.rsqrt(jnp.mean(o * o, axis=-1, keepdims=True) + EPS)
        o = o * (sg_ref[...] * (1.0 - LAMBDA_INIT))
        o_ref[0, :, sl] = (o * gd_ref[0, :, sl].astype(F32)).astype(BF16)


def _attn_sample(q, k_new, v_new, k_past_t, v_past, gd, subln_g, lam_params):
    b, ts, _ = q.shape
    past = k_past_t.shape[2]
    new = pl.BlockSpec((1, ts, DA_WIDTH), lambda i: (i, 0, 0))
    new_v = pl.BlockSpec((1, ts * DA_HEADS, DA_VDIM), lambda i: (i, 0, 0))
    old_kt = pl.BlockSpec((1, DA_QK, past), lambda i: (i, 0, 0))
    old_v = pl.BlockSpec((1, past * DA_HEADS, DA_VDIM), lambda i: (i, 0, 0))
    return pl.pallas_call(
        functools.partial(_attn_sample_kernel, past),
        grid=(b,),
        in_specs=[new, new, new_v, old_kt, old_v, new,
                  pl.BlockSpec((1, DA_VDIM), lambda i: (0, 0)),
                  pl.BlockSpec((4, DA_HEAD_DIM), lambda i: (0, 0))],
        out_specs=new,
        out_shape=jax.ShapeDtypeStruct((b, ts, DA_WIDTH), BF16),
        compiler_params=pltpu.CompilerParams(
            dimension_semantics=("arbitrary",), vmem_limit_bytes=VMEM_LIMIT_BYTES),
        name="diff_attn_sample",
    )(q, k_new, v_new, k_past_t, v_past, gd, subln_g, lam_params)


def _out_proj_kernel(x_ref, a_ref, mix_ref, w_ref, y_ref):
    y_ref[0] = (x_ref[0] + _dot(a_ref[0], w_ref[:DA_WIDTH, :]) + _dot(mix_ref[0], w_ref[DA_WIDTH:, :]))


def _out_proj(x, out_a, mix, w_out, tm):
    b, t, _ = x.shape
    tile = lambda width: pl.BlockSpec((1, tm, width), lambda i, j: (i, j, 0))
    return pl.pallas_call(
        _out_proj_kernel,
        grid=(b, t // tm),
        in_specs=[tile(D_MODEL), tile(DA_WIDTH), tile(DA_WIDTH),
                  pl.BlockSpec((D_MODEL, D_MODEL), lambda i, j: (0, 0))],
        out_specs=tile(D_MODEL),
        out_shape=jax.ShapeDtypeStruct((b, t, D_MODEL), F32),
        compiler_params=pltpu.CompilerParams(
            dimension_semantics=("arbitrary", "arbitrary"), vmem_limit_bytes=VMEM_LIMIT_BYTES),
        name="out_proj",
    )(x, out_a, mix, w_out)


def _rope_tables(pos):
    half = ROT_DIM // 2
    inv = ROPE_THETA ** (-jnp.arange(0, ROT_DIM, 2, dtype=F32) / ROT_DIM)
    ang = pos.astype(F32)[:, None] * inv[None, :]
    cos, sin = jnp.cos(ang), jnp.sin(ang)
    t = pos.shape[0]
    ones = jnp.ones((t, DA_HEAD_DIM - ROT_DIM), F32)
    zeros = jnp.zeros((t, DA_HEAD_DIM - ROT_DIM), F32)
    zh = jnp.zeros((t, half), F32)
    c = jnp.concatenate([cos, cos, ones], axis=1)
    s_up = jnp.concatenate([-sin, zh, zeros], axis=1)
    s_dn = jnp.concatenate([zh, sin, zeros], axis=1)
    rep = LANES // DA_HEAD_DIM
    return tuple(jnp.tile(a, (1, rep)) for a in (c, s_up, s_dn))


def _block_diag(w):
    n, bi, bj = w.shape
    eye = jnp.eye(n, dtype=w.dtype)
    return (eye[:, None, :, None] * w[:, :, None, :]).reshape(n * bi, n * bj)


def _pick_tile(t, pref):
    return pref if t % pref == 0 else t


def kernel(x_prompt, x_sample, mem_prompt, cache_diff_k, cache_diff_v, cache_mem_k, cache_mem_v,
           state_lru_conv, state_lru_h, norm_g, w_in, da_q_norm_g, da_k_norm_g, lambda_q1, lambda_k1,
           lambda_q2, lambda_k2, da_subln_g, lru_conv_w, lru_conv_b, lru_w_a, lru_b_a, lru_w_x, lru_b_x,
           lru_lambda, mem_norm_g, w_mem_kv, mx_q_norm_g, mx_k_norm_g, w_out):
    depth = w_in.shape[0]
    assert depth == 1
    bp, tp, _ = x_prompt.shape
    bs, ts, _ = x_sample.shape
    past = cache_diff_k.shape[2]
    l = 0
    grp = np.arange(MXU_TILE) // DA_HEAD_DIM
    ones_bd = jnp.asarray(grp[:, None] == grp[None, :], BF16)
    wts = dict(
        w_in=w_in[l].astype(BF16), norm_g=norm_g[l][None, :],
        gq=jnp.tile(da_q_norm_g[l], DA_QK // DA_HEAD_DIM)[None, :],
        gk=jnp.tile(da_k_norm_g[l], DA_QK // DA_HEAD_DIM)[None, :],
        gmq=jnp.tile(mx_q_norm_g[l], MX_HEADS)[None, :], ones_bd=ones_bd,
        conv_w=lru_conv_w[l], conv_b=lru_conv_b[l][None, :],
        wa=_block_diag(lru_w_a[l]).astype(BF16), ba=lru_b_a[l][None, :],
        wx=_block_diag(lru_w_x[l]).astype(BF16), bx=lru_b_x[l][None, :],
        lru_lambda=lru_lambda[l][None, :])
    lam_params = jnp.stack([lambda_q1[l], lambda_k1[l], lambda_q2[l], lambda_k2[l]])
    subln_g = da_subln_g[l][None, :]
    w_out_b = w_out[l].astype(BF16)

    mk_p, mv_p = _mem_kv(mem_prompt, mem_norm_g[l][None, :], w_mem_kv[l].astype(BF16),
                         jnp.tile(mx_k_norm_g[l], MX_HEADS)[None, :], ones_bd)
    tm_p = _pick_tile(tp, 512)
    tq_p = _pick_tile(tp, 512)
    pos_p = jnp.arange(tp, dtype=jnp.int32)
    q_p, k_p, v_p, gd_p, mix_p, c_p, h_p, kb_p, vt_p = _in_proj(
        x_prompt, jnp.zeros((bp, CONV_W - 1, LRU_WIDTH), F32), jnp.zeros((bp, 1, LRU_WIDTH), F32),
        mk_p, mv_p, wts, _rope_tables(pos_p), tm_p, True)
    y_p = _attn_prompt(q_p, kb_p, vt_p, gd_p, subln_g, lam_params, x_prompt, mix_p, w_out_b,
                       tq_p, tm_p, ATTN_HEADS_PER_STEP)

    pos_s = past + jnp.arange(ts, dtype=jnp.int32)
    q_s, k_s, v_s, gd_s, mix_s, c_s, h_s = _in_proj(
        x_sample, state_lru_conv[l], state_lru_h[l][:, None, :],
        cache_mem_k[l].reshape(bs, N_MEM, MX_WIDTH), cache_mem_v[l].reshape(bs, N_MEM, MX_WIDTH),
        wts, _rope_tables(pos_s), ts, False)
    oa_s = _attn_sample(q_s, k_s, v_s, jnp.swapaxes(cache_diff_k[l].reshape(bs, past, DA_QK), 1, 2),
                        cache_diff_v[l].reshape(bs, past * DA_HEADS, DA_VDIM), gd_s, subln_g, lam_params)
    y_s = _out_proj(x_sample, oa_s, mix_s, w_out_b, ts)

    return (y_p, y_s,
            k_p.reshape(1, bp, tp, DA_HEADS, 2, DA_HEAD_DIM), v_p.reshape(1, bp, tp, DA_HEADS, DA_VDIM),
            mk_p.reshape(1, bp, N_MEM, MX_HEADS, MX_HEAD_DIM), mv_p.reshape(1, bp, N_MEM, MX_HEADS, MX_HEAD_DIM),
            c_p[None], h_p.reshape(1, bp, LRU_WIDTH),
            k_s.reshape(1, bs, ts, DA_HEADS, 2, DA_HEAD_DIM), v_s.reshape(1, bs, ts, DA_HEADS, DA_VDIM),
            c_s[None], h_s.reshape(1, bs, LRU_WIDTH))
```

```python
import functools
import math

import numpy as np
import jax
import jax.numpy as jnp
from jax import lax
from jax.experimental import pallas as pl
from jax.experimental.pallas import tpu as pltpu

D_MODEL = 1024
CHUNK = 64
EPS = 1e-6
DA_HEADS = 4
DA_HEAD_DIM = 64
DA_VDIM = 128
DA_QK = 512
DA_WIDTH = 512
ROPE_THETA = 500000.0
ROT_DIM = 16
LRU_WIDTH = 256
LRU_BLOCKS = 4
CONV_W = 4
LRU_C = 8.0
MX_HEADS = 4
MX_HEAD_DIM = 64
MX_WIDTH = 256
N_MEM = 256
LAMBDA_INIT = 0.8 - 0.6 * math.exp(-0.3 * 0)
NEG_BIG = -1e30
LOG2E = math.log2(math.e)

OFF_DQ, OFF_DK, OFF_DV, OFF_DG = 0, 512, 1024, 1536
OFF_LX, OFF_LG, OFF_MQ, OFF_MG = 2048, 2304, 2560, 2816
IN_WIDTH = 3072

VMEM_LIMIT_BYTES = 52 * 1024 * 1024
LANES = 128
MXU_TILE = 256
CONV_PAD = 8
ATTN_HEADS_PER_STEP = 4
SCAN_CHUNK = 32
VT_ONES = 16

F32 = jnp.float32
BF16 = jnp.bfloat16


def _dot(a, b):
    return jnp.dot(a, b, preferred_element_type=F32)


def _dot_nt(a, b):
    return lax.dot_general(a, b, (((1,), (1,)), ((), ())), preferred_element_type=F32)


def _sigmoid(x):
    return 0.5 + 0.5 * jnp.tanh(0.5 * x)


def _silu(x):
    return x * _sigmoid(x)


def _group_rms_scale(x, ones_bd, group):
    xsq = (x * x).astype(BF16)
    wb = ones_bd.shape[0]
    ss = jnp.concatenate([_dot(xsq[:, c:c + wb], ones_bd) for c in range(0, x.shape[1], wb)], axis=1)
    return lax.rsqrt(ss * (1.0 / group) + EPS)


def _rope(x, c, s_up, s_dn):
    w = x.shape[1]
    return x * c + pltpu.roll(x, w - ROT_DIM // 2, 1) * s_up + pltpu.roll(x, ROT_DIM // 2, 1) * s_dn


def _mem_kv_kernel(mem_ref, g_ref, w_ref, gk_ref, ones_ref, k_ref, v_ref):
    x = mem_ref[0]
    y = x * lax.rsqrt(jnp.mean(x * x, axis=-1, keepdims=True) + EPS) * g_ref[...]
    kv = _dot(y.astype(BF16), w_ref[...])
    k = kv[:, :MX_WIDTH]
    k_ref[0] = (k * _group_rms_scale(k, ones_ref[...], MX_HEAD_DIM) * gk_ref[...]).T
    v_ref[0] = kv[:, MX_WIDTH:].T


def _mem_kv(mem, mem_norm_g, w_mem_kv, gk_tiled, ones256):
    b, n, _ = mem.shape
    full = lambda shape: pl.BlockSpec(shape, lambda i: (0,) * len(shape))
    return pl.pallas_call(
        _mem_kv_kernel,
        grid=(b,),
        in_specs=[
            pl.BlockSpec((1, n, D_MODEL), lambda i: (i, 0, 0)),
            full((1, D_MODEL)),
            full((D_MODEL, 2 * MX_WIDTH)),
            full((1, MX_WIDTH)),
            full((MX_WIDTH, MX_WIDTH)),
        ],
        out_specs=[pl.BlockSpec((1, MX_WIDTH, n), lambda i: (i, 0, 0))] * 2,
        out_shape=[jax.ShapeDtypeStruct((b, MX_WIDTH, n), F32)] * 2,
        compiler_params=pltpu.CompilerParams(
            dimension_semantics=("arbitrary",), vmem_limit_bytes=VMEM_LIMIT_BYTES),
        name="mem_kv",
    )(mem, mem_norm_g, w_mem_kv, gk_tiled, ones256)


def _lru_scan(a, b, h_prev):
    tm = a.shape[0]
    chunk = min(tm, SCAN_CHUNK)
    row = lax.broadcasted_iota(jnp.int32, (chunk, a.shape[1]), 0)
    out = []
    for c0 in range(0, tm, chunk):
        ac, bc = a[c0:c0 + chunk], b[c0:c0 + chunk]
        d = 1
        while d < chunk:
            keep = row >= d
            bc = ac * jnp.where(keep, pltpu.roll(bc, d, 0), 0.0) + bc
            ac = ac * jnp.where(keep, pltpu.roll(ac, d, 0), 1.0)
            d *= 2
        hc = bc + ac * h_prev
        h_prev = hc[chunk - 1:chunk, :]
        out.append(hc)
    return jnp.concatenate(out, axis=0)


def _in_proj_kernel(emit_attn_layouts, tm,
                    x_ref, cbuf_ref, h0_ref, w_ref, ng_ref, gq_ref, gk_ref, gmq_ref, ones_ref,
                    rc_ref, ru_ref, rd_ref, cw_ref, cb_ref, wa_ref, ba_ref, wx_ref, bx_ref,
                    lam_ref, mk_ref, mv_ref, *rest):
    if emit_attn_layouts:
        (q_ref, k_ref, v_ref, gd_ref, mix_ref, cout_ref, hout_ref, kb_ref, vt_ref,
         xbuf, hc) = rest
    else:
        q_ref, k_ref, v_ref, gd_ref, mix_ref, cout_ref, hout_ref, xbuf, hc = rest
    t = pl.program_id(1)

    @pl.when(t == 0)
    def _():
        xbuf[pl.ds(CONV_PAD - (CONV_W - 1), CONV_W - 1), :] = cbuf_ref[0]
        hc[...] = h0_ref[0]

    x = x_ref[0]
    hn = (x * lax.rsqrt(jnp.mean(x * x, axis=-1, keepdims=True) + EPS) * ng_ref[...]).astype(BF16)

    def proj(off, width):
        return _dot(hn, w_ref[:, off:off + width])

    rc = jnp.concatenate([rc_ref[...]] * (DA_QK // LANES), axis=1)
    ru = jnp.concatenate([ru_ref[...]] * (DA_QK // LANES), axis=1)
    rd = jnp.concatenate([rd_ref[...]] * (DA_QK // LANES), axis=1)
    ones_bd = ones_ref[...]

    dq = proj(OFF_DQ, DA_QK)
    q = _rope(dq * _group_rms_scale(dq, ones_bd, DA_HEAD_DIM) * gq_ref[...], rc, ru, rd)
    q_ref[0] = (q * (DA_HEAD_DIM ** -0.5 * LOG2E)).astype(BF16)
    dk = proj(OFF_DK, DA_QK)
    k = _rope(dk * _group_rms_scale(dk, ones_bd, DA_HEAD_DIM) * gk_ref[...], rc, ru, rd)
    k_ref[0] = k
    dv = proj(OFF_DV, DA_WIDTH)
    for h in range(DA_HEADS):
        v_ref[0, pl.ds(h, tm, stride=DA_HEADS), :] = dv[:, h * DA_VDIM:(h + 1) * DA_VDIM]
    if emit_attn_layouts:
        kb_ref[0] = k.astype(BF16)
        for h in range(DA_HEADS):
            vt_ref[0, h, 0, :DA_VDIM, :] = dv[:, h * DA_VDIM:(h + 1) * DA_VDIM].T.astype(BF16)
            vt_ref[0, h, 0, DA_VDIM:, :] = jnp.ones((VT_ONES, tm), BF16)
    gd_ref[0] = _silu(proj(OFF_DG, DA_WIDTH)).astype(BF16)

    lx = proj(OFF_LX, LRU_WIDTH)
    xbuf[pl.ds(CONV_PAD, tm), :] = lx
    xc = cb_ref[...]
    for j in range(CONV_W):
        xc = xc + xbuf[pl.ds(CONV_PAD - (CONV_W - 1) + j, tm), :] * cw_ref[j:j + 1, :]
    tail = xbuf[pl.ds(CONV_PAD + tm - (CONV_W - 1), CONV_W - 1), :]
    cout_ref[0] = tail
    xbuf[pl.ds(CONV_PAD - (CONV_W - 1), CONV_W - 1), :] = tail
    xcb = xc.astype(BF16)
    r = _sigmoid(_dot(xcb, wa_ref[...]) + ba_ref[...])
    i = _sigmoid(_dot(xcb, wx_ref[...]) + bx_ref[...])
    neg_lam = -lam_ref[...]
    softplus = jnp.maximum(neg_lam, 0.0) + jnp.log1p(jnp.exp(-jnp.abs(neg_lam)))
    log_a = (-LRU_C) * r * softplus
    a = jnp.exp(log_a)
    bb = (1.0 + a) * jnp.sqrt(-jnp.tanh(0.5 * log_a)) * (i * xc)
    hseq = _lru_scan(a, bb, hc[...])
    h_last = hseq[tm - 1:tm, :]
    hc[...] = h_last
    hout_ref[0] = h_last
    out_b = hseq * _silu(proj(OFF_LG, LRU_WIDTH))

    mq = proj(OFF_MQ, MX_WIDTH)
    qm = (mq * _group_rms_scale(mq, ones_bd, MX_HEAD_DIM) * gmq_ref[...]
          * (MX_HEAD_DIM ** -0.5 * LOG2E)).astype(BF16)
    mk_t = mk_ref[0].astype(BF16)
    mv_t = mv_ref[0].astype(BF16)
    lane_q = lax.broadcasted_iota(jnp.int32, (tm, LANES), 1)
    feat_v = lax.broadcasted_iota(jnp.int32, (LANES, N_MEM), 0)
    slabs = []
    for sl in range(MX_WIDTH // LANES):
        qs = qm[:, sl * LANES:(sl + 1) * LANES]
        ks_t = mk_t[sl * LANES:(sl + 1) * LANES, :]
        vs_t = mv_t[sl * LANES:(sl + 1) * LANES, :]
        acc = None
        rl = None
        for half in range(2):
            sel_q = (lane_q >= MX_HEAD_DIM) if half else (lane_q < MX_HEAD_DIM)
            sel_v = (feat_v >= MX_HEAD_DIM) if half else (feat_v < MX_HEAD_DIM)
            s = _dot(jnp.where(sel_q, qs, jnp.zeros_like(qs)), ks_t)
            p = jnp.exp2(s - jnp.max(s, axis=-1, keepdims=True))
            rsum = 1.0 / jnp.sum(p, axis=-1, keepdims=True)
            o = _dot_nt(p.astype(BF16), jnp.where(sel_v, vs_t, jnp.zeros_like(vs_t)))
            acc = o if acc is None else acc + o
            rl = rsum if rl is None else jnp.where(sel_q, rsum, rl)
        slabs.append(acc * rl)
    om = jnp.concatenate(slabs, axis=1)
    out_c = om * _silu(proj(OFF_MG, MX_WIDTH))
    mix_ref[0] = jnp.concatenate([out_b, out_c], axis=1).astype(BF16)


def _in_proj(x, cbuf, h0, mk, mv, wts, rope_tabs, tm, emit_attn_layouts):
    b, t, _ = x.shape
    nt = t // tm
    full = lambda shape: pl.BlockSpec(shape, lambda i, j: (0,) * len(shape))
    per_b = lambda shape: pl.BlockSpec(shape, lambda i, j: (i,) + (0,) * (len(shape) - 1))
    tile = lambda width: pl.BlockSpec((1, tm, width), lambda i, j: (i, j, 0))
    rope_spec = pl.BlockSpec((tm, LANES), lambda i, j: (j, 0))
    in_specs = [
        tile(D_MODEL), per_b((1, CONV_W - 1, LRU_WIDTH)), per_b((1, 1, LRU_WIDTH)),
        full((D_MODEL, IN_WIDTH)), full((1, D_MODEL)), full((1, DA_QK)), full((1, DA_QK)),
        full((1, MX_WIDTH)), full((MXU_TILE, MXU_TILE)),
        rope_spec, rope_spec, rope_spec,
        full((CONV_W, LRU_WIDTH)), full((1, LRU_WIDTH)),
        full((LRU_WIDTH, LRU_WIDTH)), full((1, LRU_WIDTH)),
        full((LRU_WIDTH, LRU_WIDTH)), full((1, LRU_WIDTH)), full((1, LRU_WIDTH)),
        per_b((1, MX_WIDTH, N_MEM)), per_b((1, MX_WIDTH, N_MEM)),
    ]
    out_specs = [tile(DA_QK), tile(DA_QK),
                 pl.BlockSpec((1, tm * DA_HEADS, DA_VDIM), lambda i, j: (i, j, 0)),
                 tile(DA_WIDTH), tile(DA_WIDTH),
                 per_b((1, CONV_W - 1, LRU_WIDTH)), per_b((1, 1, LRU_WIDTH))]
    out_shape = [jax.ShapeDtypeStruct((b, t, DA_QK), BF16),
                 jax.ShapeDtypeStruct((b, t, DA_QK), F32),
                 jax.ShapeDtypeStruct((b, t * DA_HEADS, DA_VDIM), F32),
                 jax.ShapeDtypeStruct((b, t, DA_WIDTH), BF16),
                 jax.ShapeDtypeStruct((b, t, DA_WIDTH), BF16),
                 jax.ShapeDtypeStruct((b, CONV_W - 1, LRU_WIDTH), F32),
                 jax.ShapeDtypeStruct((b, 1, LRU_WIDTH), F32)]
    if emit_attn_layouts:
        out_specs += [tile(DA_QK),
                      pl.BlockSpec((1, DA_HEADS, 1, DA_VDIM + VT_ONES, tm), lambda i, j: (i, 0, j, 0, 0))]
        out_shape += [jax.ShapeDtypeStruct((b, t, DA_QK), BF16),
                      jax.ShapeDtypeStruct((b, DA_HEADS, nt, DA_VDIM + VT_ONES, tm), BF16)]
    return pl.pallas_call(
        functools.partial(_in_proj_kernel, emit_attn_layouts, tm),
        grid=(b, nt),
        in_specs=in_specs,
        out_specs=out_specs,
        out_shape=out_shape,
        scratch_shapes=[pltpu.VMEM((CONV_PAD + tm, LRU_WIDTH), F32), pltpu.VMEM((1, LRU_WIDTH), F32)],
        compiler_params=pltpu.CompilerParams(
            dimension_semantics=("arbitrary", "arbitrary"), vmem_limit_bytes=VMEM_LIMIT_BYTES),
        name="in_proj_prompt" if emit_attn_layouts else "in_proj_sample",
    )(x, cbuf, h0, wts["w_in"], wts["norm_g"], wts["gq"], wts["gk"], wts["gmq"], wts["ones_bd"],
      *rope_tabs, wts["conv_w"], wts["conv_b"], wts["wa"], wts["ba"], wts["wx"], wts["bx"],
      wts["lru_lambda"], mk, mv)


def _lambda_value(lam_ref):
    lp = lam_ref[...]
    s1 = jnp.sum(lp[0:1] * lp[1:2], axis=-1, keepdims=True)
    s2 = jnp.sum(lp[2:3] * lp[3:4], axis=-1, keepdims=True)
    return jnp.exp(s1) - jnp.exp(s2) + LAMBDA_INIT


def _attn_prompt_kernel(tq, tk, hb, q_ref, qn_ref, k_ref, feat_ref, vt_ref, gd_ref, sg_ref, lam_ref,
                        x_ref, mix_ref, wo_ref, y_ref, w_scr, s_scr, tmax_scr, m_scr, acc_scr):
    qi = pl.program_id(2)
    cw = 2 * tq
    lane = lax.broadcasted_iota(jnp.int32, (tq, LANES), 1)
    row = lax.broadcasted_iota(jnp.int32, (tq, LANES), 0)

    def build_w(slot, src_ref, q_tile):
        qfeat = jnp.where(lane > ((q_tile * tq + row) >> 6), NEG_BIG, 0.0).astype(BF16)
        for h in range(hb):
            q = src_ref[0, :, h * LANES:(h + 1) * LANES]
            zero = jnp.zeros_like(q)
            w_scr[slot, h] = jnp.concatenate(
                [jnp.concatenate([jnp.where(lane < DA_HEAD_DIM, q, zero), qfeat], axis=1),
                 jnp.concatenate([jnp.where(lane >= DA_HEAD_DIM, q, zero), qfeat], axis=1)], axis=0)

    cur = qi % 2
    nxt = 1 - cur
    n = ((qi + 1) * tq + tk - 1) // tk

    def key_rows(j, h):
        off = pl.multiple_of(j * tk, tk)
        return jnp.concatenate([k_ref[0, pl.ds(off, tk), h * LANES:(h + 1) * LANES],
                                feat_ref[pl.ds(off, tk), :]], axis=1)

    def stage_scores(kx, slot, h, cc):
        cols = slice(h * cw + cc * MXU_TILE, h * cw + (cc + 1) * MXU_TILE)
        s = _dot_nt(kx, w_scr[slot, h, cc * MXU_TILE:(cc + 1) * MXU_TILE, :])
        s_scr[:, cols] = s
        tmax_scr[:, cols] = jnp.max(s, axis=0, keepdims=True)

    @pl.when(qi == 0)
    def _():
        build_w(0, q_ref, qi)
        for h in range(hb):
            kx = key_rows(0, h)
            for cc in range(cw // MXU_TILE):
                stage_scores(kx, 0, h, cc)

    build_w(nxt, qn_ref, qi + 1)

    m_scr[...] = jnp.full(m_scr.shape, NEG_BIG, F32)
    acc_scr[...] = jnp.zeros(acc_scr.shape, F32)

    def body(j, c):
        m_old = m_scr[...]
        m_new = jnp.maximum(m_old, tmax_scr[...])
        alpha = jnp.exp2(m_old - m_new)
        m_scr[...] = m_new
        last = j + 1 >= n
        j_next = jnp.where(last, 0, j + 1)
        slot = jnp.where(last, nxt, cur)
        for h in range(hb):
            kx = key_rows(j_next, h)
            for cc in range(cw // MXU_TILE):
                cols = slice(h * cw + cc * MXU_TILE, h * cw + (cc + 1) * MXU_TILE)
                acc_c = alpha[:, cols] * acc_scr[:, cols]
                for r in range(tk // MXU_TILE):
                    rows = slice(r * MXU_TILE, (r + 1) * MXU_TILE)
                    p = jnp.exp2(s_scr[rows, cols] - m_new[:, cols]).astype(BF16)
                    acc_c = acc_c + _dot(vt_ref[0, h, j, :, rows], p)
                acc_scr[:, cols] = acc_c
                stage_scores(kx, slot, h, cc)
        return c

    lax.fori_loop(0, n, body, 0)

    y_ref[0] = x_ref[0] + _dot(mix_ref[0], wo_ref[DA_WIDTH:, :])
    acc = acc_scr[...]
    acc = acc[:DA_VDIM] * (1.0 / acc[DA_VDIM:DA_VDIM + 1])
    lam = _lambda_value(lam_ref)
    heads = []
    for h in range(hb):
        ot = acc[:, h * cw:h * cw + tq] - lam * acc[:, h * cw + tq:(h + 1) * cw]
        ot = ot * lax.rsqrt(jnp.mean(ot * ot, axis=0, keepdims=True) + EPS)
        o = ot.T * (sg_ref[...] * (1.0 - LAMBDA_INIT))
        heads.append((o * gd_ref[0, :, h * LANES:(h + 1) * LANES].astype(F32)).astype(BF16))
    out_a = jnp.concatenate(heads, axis=1)
    y_ref[0] += _dot(out_a, wo_ref[:DA_WIDTH, :])


def _attn_prompt(q, kb, vt, gd, subln_g, lam_params, x, mix, w_out, tq, tk, hb):
    b, t, _ = q.shape
    assert hb == DA_HEADS, "the fused output projection needs every head of a query tile in one step"
    nk, nq = t // tk, t // tq
    assert t % CHUNK == 0 and t // CHUNK <= LANES, "chunk one-hot must fit one 128-lane slab"
    assert DA_HEADS % hb == 0
    key_chunk = np.arange(t) // CHUNK
    feat = jnp.asarray(key_chunk[:, None] == np.arange(LANES)[None, :], BF16)
    hw = hb * LANES
    vrows = DA_VDIM + VT_ONES
    once = dict(pipeline_mode=pl.Buffered(1))
    return pl.pallas_call(
        functools.partial(_attn_prompt_kernel, tq, tk, hb),
        grid=(b, DA_HEADS // hb, nq),
        in_specs=[
            pl.BlockSpec((1, tq, hw), lambda i, h, j: (i, j, h)),
            pl.BlockSpec((1, tq, hw), lambda i, h, j: (i, jnp.minimum(j + 1, nq - 1), h)),
            pl.BlockSpec((1, t, hw), lambda i, h, j: (i, 0, h), **once),
            pl.BlockSpec((t, LANES), lambda i, h, j: (0, 0), **once),
            pl.BlockSpec((1, hb, nk, vrows, tk), lambda i, h, j: (i, h, 0, 0, 0), **once),
            pl.BlockSpec((1, tq, hw), lambda i, h, j: (i, j, h)),
            pl.BlockSpec((1, DA_VDIM), lambda i, h, j: (0, 0)),
            pl.BlockSpec((4, DA_HEAD_DIM), lambda i, h, j: (0, 0)),
            pl.BlockSpec((1, tq, D_MODEL), lambda i, h, j: (i, j, 0)),
            pl.BlockSpec((1, tq, DA_WIDTH), lambda i, h, j: (i, j, 0)),
            pl.BlockSpec((D_MODEL, D_MODEL), lambda i, h, j: (0, 0), **once),
        ],
        out_specs=pl.BlockSpec((1, tq, D_MODEL), lambda i, h, j: (i, j, 0)),
        out_shape=jax.ShapeDtypeStruct((b, t, D_MODEL), F32),
        scratch_shapes=[pltpu.VMEM((2, hb, 2 * tq, 2 * LANES), BF16),
                        pltpu.VMEM((tk, hb * 2 * tq), F32),
                        pltpu.VMEM((1, hb * 2 * tq), F32), pltpu.VMEM((1, hb * 2 * tq), F32),
                        pltpu.VMEM((vrows, hb * 2 * tq), F32)],
        compiler_params=pltpu.CompilerParams(
            dimension_semantics=("arbitrary", "arbitrary", "arbitrary"),
            vmem_limit_bytes=VMEM_LIMIT_BYTES),
        name="diff_attn_prompt",
    )(q, q, kb, feat, vt, gd, subln_g, lam_params, x, mix, w_out)


def _attn_sample_kernel(past, q_ref, kn_ref, vn_ref, kp_ref, vp_ref, gd_ref, sg_ref, lam_ref, o_ref):
    ts = q_ref.shape[1]
    lam = _lambda_value(lam_ref)
    lane = lax.broadcasted_iota(jnp.int32, (ts, LANES), 1)
    qrow = past + lax.broadcasted_iota(jnp.int32, (ts, past), 0)
    vis_p = (lax.broadcasted_iota(jnp.int32, (ts, past), 1) >> 6) <= (qrow >> 6)
    qrow_n = past + lax.broadcasted_iota(jnp.int32, (ts, ts), 0)
    vis_n = ((past + lax.broadcasted_iota(jnp.int32, (ts, ts), 1)) >> 6) <= (qrow_n >> 6)
    for h in range(DA_HEADS):
        sl = slice(h * LANES, (h + 1) * LANES)
        q = q_ref[0, :, sl]
        kp_t = kp_ref[0, sl, :].astype(BF16)
        kn = kn_ref[0, :, sl].astype(BF16)
        zero = jnp.zeros_like(q)
        wp, wn = [], []
        for c in range(2):
            qc = jnp.where((lane >= DA_HEAD_DIM) if c else (lane < DA_HEAD_DIM), q, zero)
            sp = jnp.where(vis_p, _dot(qc, kp_t), NEG_BIG)
            sn = jnp.where(vis_n, _dot_nt(qc, kn), NEG_BIG)
            m = jnp.maximum(jnp.max(sp, axis=-1, keepdims=True), jnp.max(sn, axis=-1, keepdims=True))
            pp = jnp.exp2(sp - m)
            pn = jnp.exp2(sn - m)
            inv = 1.0 / (jnp.sum(pp, axis=-1, keepdims=True) + jnp.sum(pn, axis=-1, keepdims=True))
            wp.append(pp * inv)
            wn.append(pn * inv)
        wgt_p = (wp[0] - lam * wp[1]).astype(BF16)
        wgt_n = (wn[0] - lam * wn[1]).astype(BF16)
        v_past = vp_ref[0, pl.ds(h, past, stride=DA_HEADS), :].astype(BF16)
        v_new = vn_ref[0, pl.ds(h, ts, stride=DA_HEADS), :].astype(BF16)
        o = _dot(wgt_p, v_past) + _dot(wgt_n, v_new)
        o = o * lax.rsqrt(jnp.mean(o * o, axis=-1, keepdims=True) + EPS)
        o = o * (sg_ref[...] * (1.0 - LAMBDA_INIT))
        o_ref[0, :, sl] = (o * gd_ref[0, :, sl].astype(F32)).astype(BF16)


def _attn_sample(q, k_new, v_new, k_past_t, v_past, gd, subln_g, lam_params):
    b, ts, _ = q.shape
    past = k_past_t.shape[2]
    new = pl.BlockSpec((1, ts, DA_WIDTH), lambda i: (i, 0, 0))
    new_v = pl.BlockSpec((1, ts * DA_HEADS, DA_VDIM), lambda i: (i, 0, 0))
    old_kt = pl.BlockSpec((1, DA_QK, past), lambda i: (i, 0, 0))
    old_v = pl.BlockSpec((1, past * DA_HEADS, DA_VDIM), lambda i: (i, 0, 0))
    return pl.pallas_call(
        functools.partial(_attn_sample_kernel, past),
        grid=(b,),
        in_specs=[new, new, new_v, old_kt, old_v, new,
                  pl.BlockSpec((1, DA_VDIM), lambda i: (0, 0)),
                  pl.BlockSpec((4, DA_HEAD_DIM), lambda i: (0, 0))],
        out_specs=new,
        out_shape=jax.ShapeDtypeStruct((b, ts, DA_WIDTH), BF16),
        compiler_params=pltpu.CompilerParams(
            dimension_semantics=("arbitrary",), vmem_limit_bytes=VMEM_LIMIT_BYTES),
        name="diff_attn_sample",
    )(q, k_new, v_new, k_past_t, v_past, gd, subln_g, lam_params)


def _out_proj_kernel(x_ref, a_ref, mix_ref, w_ref, y_ref):
    y_ref[0] = (x_ref[0] + _dot(a_ref[0], w_ref[:DA_WIDTH, :]) + _dot(mix_ref[0], w_ref[DA_WIDTH:, :]))


def _out_proj(x, out_a, mix, w_out, tm):
    b, t, _ = x.shape
    tile = lambda width: pl.BlockSpec((1, tm, width), lambda i, j: (i, j, 0))
    return pl.pallas_call(
        _out_proj_kernel,
        grid=(b, t // tm),
        in_specs=[tile(D_MODEL), tile(DA_WIDTH), tile(DA_WIDTH),
                  pl.BlockSpec((D_MODEL, D_MODEL), lambda i, j: (0, 0))],
        out_specs=tile(D_MODEL),
        out_shape=jax.ShapeDtypeStruct((b, t, D_MODEL), F32),
        compiler_params=pltpu.CompilerParams(
            dimension_semantics=("arbitrary", "arbitrary"), vmem_limit_bytes=VMEM_LIMIT_BYTES),
        name="out_proj",
    )(x, out_a, mix, w_out)


def _rope_tables(pos):
    half = ROT_DIM // 2
    inv = ROPE_THETA ** (-jnp.arange(0, ROT_DIM, 2, dtype=F32) / ROT_DIM)
    ang = pos.astype(F32)[:, None] * inv[None, :]
    cos, sin = jnp.cos(ang), jnp.sin(ang)
    t = pos.shape[0]
    ones = jnp.ones((t, DA_HEAD_DIM - ROT_DIM), F32)
    zeros = jnp.zeros((t, DA_HEAD_DIM - ROT_DIM), F32)
    zh = jnp.zeros((t, half), F32)
    c = jnp.concatenate([cos, cos, ones], axis=1)
    s_up = jnp.concatenate([-sin, zh, zeros], axis=1)
    s_dn = jnp.concatenate([zh, sin, zeros], axis=1)
    rep = LANES // DA_HEAD_DIM
    return tuple(jnp.tile(a, (1, rep)) for a in (c, s_up, s_dn))


def _block_diag(w):
    n, bi, bj = w.shape
    eye = jnp.eye(n, dtype=w.dtype)
    return (eye[:, None, :, None] * w[:, :, None, :]).reshape(n * bi, n * bj)


def _pick_tile(t, pref):
    return pref if t % pref == 0 else t


def kernel(x_prompt, x_sample, mem_prompt, cache_diff_k, cache_diff_v, cache_mem_k, cache_mem_v,
           state_lru_conv, state_lru_h, norm_g, w_in, da_q_norm_g, da_k_norm_g, lambda_q1, lambda_k1,
           lambda_q2, lambda_k2, da_subln_g, lru_conv_w, lru_conv_b, lru_w_a, lru_b_a, lru_w_x, lru_b_x,
           lru_lambda, mem_norm_g, w_mem_kv, mx_q_norm_g, mx_k_norm_g, w_out):
    depth = w_in.shape[0]
    assert depth == 1
    bp, tp, _ = x_prompt.shape
    bs, ts, _ = x_sample.shape
    past = cache_diff_k.shape[2]
    l = 0
    grp = np.arange(MXU_TILE) // DA_HEAD_DIM
    ones_bd = jnp.asarray(grp[:, None] == grp[None, :], BF16)
    wts = dict(
        w_in=w_in[l].astype(BF16), norm_g=norm_g[l][None, :],
        gq=jnp.tile(da_q_norm_g[l], DA_QK // DA_HEAD_DIM)[None, :],
        gk=jnp.tile(da_k_norm_g[l], DA_QK // DA_HEAD_DIM)[None, :],
        gmq=jnp.tile(mx_q_norm_g[l], MX_HEADS)[None, :], ones_bd=ones_bd,
        conv_w=lru_conv_w[l], conv_b=lru_conv_b[l][None, :],
        wa=_block_diag(lru_w_a[l]).astype(BF16), ba=lru_b_a[l][None, :],
        wx=_block_diag(lru_w_x[l]).astype(BF16), bx=lru_b_x[l][None, :],
        lru_lambda=lru_lambda[l][None, :])
    lam_params = jnp.stack([lambda_q1[l], lambda_k1[l], lambda_q2[l], lambda_k2[l]])
    subln_g = da_subln_g[l][None, :]
    w_out_b = w_out[l].astype(BF16)

    mk_p, mv_p = _mem_kv(mem_prompt, mem_norm_g[l][None, :], w_mem_kv[l].astype(BF16),
                         jnp.tile(mx_k_norm_g[l], MX_HEADS)[None, :], ones_bd)
    tm_p = _pick_tile(tp, 512)
    tq_p = _pick_tile(tp, 512)
    pos_p = jnp.arange(tp, dtype=jnp.int32)
    q_p, k_p, v_p, gd_p, mix_p, c_p, h_p, kb_p, vt_p = _in_proj(
        x_prompt, jnp.zeros((bp, CONV_W - 1, LRU_WIDTH), F32), jnp.zeros((bp, 1, LRU_WIDTH), F32),
        mk_p, mv_p, wts, _rope_tables(pos_p), tm_p, True)
    y_p = _attn_prompt(q_p, kb_p, vt_p, gd_p, subln_g, lam_params, x_prompt, mix_p, w_out_b,
                       tq_p, tm_p, ATTN_HEADS_PER_STEP)

    pos_s = past + jnp.arange(ts, dtype=jnp.int32)
    q_s, k_s, v_s, gd_s, mix_s, c_s, h_s = _in_proj(
        x_sample, state_lru_conv[l], state_lru_h[l][:, None, :],
        jnp.swapaxes(cache_mem_k[l].reshape(bs, N_MEM, MX_WIDTH), 1, 2),
        jnp.swapaxes(cache_mem_v[l].reshape(bs, N_MEM, MX_WIDTH), 1, 2),
        wts, _rope_tables(pos_s), ts, False)
    oa_s = _attn_sample(q_s, k_s, v_s, jnp.swapaxes(cache_diff_k[l].reshape(bs, past, DA_QK), 1, 2),
                        cache_diff_v[l].reshape(bs, past * DA_HEADS, DA_VDIM), gd_s, subln_g, lam_params)
    y_s = _out_proj(x_sample, oa_s, mix_s, w_out_b, ts)

    return (y_p, y_s,
            k_p.reshape(1, bp, tp, DA_HEADS, 2, DA_HEAD_DIM), v_p.reshape(1, bp, tp, DA_HEADS, DA_VDIM),
            jnp.swapaxes(mk_p, 1, 2).reshape(1, bp, N_MEM, MX_HEADS, MX_HEAD_DIM),
            jnp.swapaxes(mv_p, 1, 2).reshape(1, bp, N_MEM, MX_HEADS, MX_HEAD_DIM),
            c_p[None], h_p.reshape(1, bp, LRU_WIDTH),
            k_s.reshape(1, bs, ts, DA_HEADS, 2, DA_HEAD_DIM), v_s.reshape(1, bs, ts, DA_HEADS, DA_VDIM),
            c_s[None], h_s.reshape(1, bs, LRU_WIDTH))
```

```python
import functools
import math

import numpy as np
import jax
import jax.numpy as jnp
from jax import lax
from jax.experimental import pallas as pl
from jax.experimental.pallas import tpu as pltpu

D_MODEL = 1024
CHUNK = 64
EPS = 1e-6
DA_HEADS = 4
DA_HEAD_DIM = 64
DA_VDIM = 128
DA_QK = 512
DA_WIDTH = 512
ROPE_THETA = 500000.0
ROT_DIM = 16
LRU_WIDTH = 256
LRU_BLOCKS = 4
CONV_W = 4
LRU_C = 8.0
MX_HEADS = 4
MX_HEAD_DIM = 64
MX_WIDTH = 256
N_MEM = 256
LAMBDA_INIT = 0.8 - 0.6 * math.exp(-0.3 * 0)
NEG_BIG = -1e30
LOG2E = math.log2(math.e)

OFF_DQ, OFF_DK, OFF_DV, OFF_DG = 0, 512, 1024, 1536
OFF_LX, OFF_LG, OFF_MQ, OFF_MG = 2048, 2304, 2560, 2816
IN_WIDTH = 3072

VMEM_LIMIT_BYTES = 52 * 1024 * 1024
LANES = 128
MXU_TILE = 256
CONV_PAD = 8
ATTN_HEADS_PER_STEP = 4
SCAN_CHUNK = 32
VT_ONES = 16

F32 = jnp.float32
BF16 = jnp.bfloat16


def _dot(a, b):
    return jnp.dot(a, b, preferred_element_type=F32)


def _dot_nt(a, b):
    return lax.dot_general(a, b, (((1,), (1,)), ((), ())), preferred_element_type=F32)


def _sigmoid(x):
    return 0.5 + 0.5 * jnp.tanh(0.5 * x)


def _silu(x):
    return x * _sigmoid(x)


def _group_rms_scale(x, ones_bd, group):
    xsq = (x * x).astype(BF16)
    wb = ones_bd.shape[0]
    ss = jnp.concatenate([_dot(xsq[:, c:c + wb], ones_bd) for c in range(0, x.shape[1], wb)], axis=1)
    return lax.rsqrt(ss * (1.0 / group) + EPS)


def _rope(x, c, s_up, s_dn):
    w = x.shape[1]
    return x * c + pltpu.roll(x, w - ROT_DIM // 2, 1) * s_up + pltpu.roll(x, ROT_DIM // 2, 1) * s_dn


def _mem_kv_kernel(mem_ref, g_ref, w_ref, gk_ref, ones_ref, k_ref, v_ref):
    x = mem_ref[0]
    y = x * lax.rsqrt(jnp.mean(x * x, axis=-1, keepdims=True) + EPS) * g_ref[...]
    kv = _dot(y.astype(BF16), w_ref[...])
    k = kv[:, :MX_WIDTH]
    k_ref[0] = (k * _group_rms_scale(k, ones_ref[...], MX_HEAD_DIM) * gk_ref[...]).T
    v_ref[0] = kv[:, MX_WIDTH:].T


def _mem_kv(mem, mem_norm_g, w_mem_kv, gk_tiled, ones256):
    b, n, _ = mem.shape
    full = lambda shape: pl.BlockSpec(shape, lambda i: (0,) * len(shape))
    return pl.pallas_call(
        _mem_kv_kernel,
        grid=(b,),
        in_specs=[
            pl.BlockSpec((1, n, D_MODEL), lambda i: (i, 0, 0)),
            full((1, D_MODEL)),
            full((D_MODEL, 2 * MX_WIDTH)),
            full((1, MX_WIDTH)),
            full((MX_WIDTH, MX_WIDTH)),
        ],
        out_specs=[pl.BlockSpec((1, MX_WIDTH, n), lambda i: (i, 0, 0))] * 2,
        out_shape=[jax.ShapeDtypeStruct((b, MX_WIDTH, n), F32)] * 2,
        compiler_params=pltpu.CompilerParams(
            dimension_semantics=("arbitrary",), vmem_limit_bytes=VMEM_LIMIT_BYTES),
        name="mem_kv",
    )(mem, mem_norm_g, w_mem_kv, gk_tiled, ones256)


def _lru_scan(a, b, h_prev):
    tm = a.shape[0]
    chunk = min(tm, SCAN_CHUNK)
    row = lax.broadcasted_iota(jnp.int32, (chunk, a.shape[1]), 0)
    out = []
    for c0 in range(0, tm, chunk):
        ac, bc = a[c0:c0 + chunk], b[c0:c0 + chunk]
        d = 1
        while d < chunk:
            keep = row >= d
            bc = ac * jnp.where(keep, pltpu.roll(bc, d, 0), 0.0) + bc
            ac = ac * jnp.where(keep, pltpu.roll(ac, d, 0), 1.0)
            d *= 2
        hc = bc + ac * h_prev
        h_prev = hc[chunk - 1:chunk, :]
        out.append(hc)
    return jnp.concatenate(out, axis=0)


def _in_proj_kernel(emit_attn_layouts, tm,
                    x_ref, cbuf_ref, h0_ref, w_ref, ng_ref, gq_ref, gk_ref, gmq_ref, ones_ref,
                    rc_ref, ru_ref, rd_ref, cw_ref, cb_ref, wa_ref, ba_ref, wx_ref, bx_ref,
                    lam_ref, mk_ref, mv_ref, *rest):
    if emit_attn_layouts:
        (q_ref, k_ref, v_ref, gd_ref, mix_ref, cout_ref, hout_ref, kb_ref, vt_ref,
         xbuf, hc) = rest
    else:
        q_ref, k_ref, v_ref, gd_ref, mix_ref, cout_ref, hout_ref, xbuf, hc = rest
    t = pl.program_id(1)

    @pl.when(t == 0)
    def _():
        xbuf[pl.ds(CONV_PAD - (CONV_W - 1), CONV_W - 1), :] = cbuf_ref[0]
        hc[...] = h0_ref[0]

    x = x_ref[0]
    hn = (x * lax.rsqrt(jnp.mean(x * x, axis=-1, keepdims=True) + EPS) * ng_ref[...]).astype(BF16)

    def proj(off, width):
        return _dot(hn, w_ref[:, off:off + width])

    rc = jnp.concatenate([rc_ref[...]] * (DA_QK // LANES), axis=1)
    ru = jnp.concatenate([ru_ref[...]] * (DA_QK // LANES), axis=1)
    rd = jnp.concatenate([rd_ref[...]] * (DA_QK // LANES), axis=1)
    ones_bd = ones_ref[...]

    dq = proj(OFF_DQ, DA_QK)
    q = _rope(dq * _group_rms_scale(dq, ones_bd, DA_HEAD_DIM) * gq_ref[...], rc, ru, rd)
    q_ref[0] = (q * (DA_HEAD_DIM ** -0.5 * LOG2E)).astype(BF16)
    dk = proj(OFF_DK, DA_QK)
    k = _rope(dk * _group_rms_scale(dk, ones_bd, DA_HEAD_DIM) * gk_ref[...], rc, ru, rd)
    k_ref[0] = k
    dv = proj(OFF_DV, DA_WIDTH)
    for h in range(DA_HEADS):
        v_ref[0, pl.ds(h, tm, stride=DA_HEADS), :] = dv[:, h * DA_VDIM:(h + 1) * DA_VDIM]
    if emit_attn_layouts:
        kb_ref[0] = k.astype(BF16)
        for h in range(DA_HEADS):
            vt_ref[0, h, 0, :DA_VDIM, :] = dv[:, h * DA_VDIM:(h + 1) * DA_VDIM].T.astype(BF16)
            vt_ref[0, h, 0, DA_VDIM:, :] = jnp.ones((VT_ONES, tm), BF16)
    gd_ref[0] = _silu(proj(OFF_DG, DA_WIDTH)).astype(BF16)

    lx = proj(OFF_LX, LRU_WIDTH)
    xbuf[pl.ds(CONV_PAD, tm), :] = lx
    xc = cb_ref[...]
    for j in range(CONV_W):
        xc = xc + xbuf[pl.ds(CONV_PAD - (CONV_W - 1) + j, tm), :] * cw_ref[j:j + 1, :]
    tail = xbuf[pl.ds(CONV_PAD + tm - (CONV_W - 1), CONV_W - 1), :]
    cout_ref[0] = tail
    xbuf[pl.ds(CONV_PAD - (CONV_W - 1), CONV_W - 1), :] = tail
    xcb = xc.astype(BF16)
    r = _sigmoid(_dot(xcb, wa_ref[...]) + ba_ref[...])
    i = _sigmoid(_dot(xcb, wx_ref[...]) + bx_ref[...])
    neg_lam = -lam_ref[...]
    softplus = jnp.maximum(neg_lam, 0.0) + jnp.log1p(jnp.exp(-jnp.abs(neg_lam)))
    log_a = (-LRU_C) * r * softplus
    a = jnp.exp(log_a)
    bb = (1.0 + a) * jnp.sqrt(-jnp.tanh(0.5 * log_a)) * (i * xc)
    hseq = _lru_scan(a, bb, hc[...])
    h_last = hseq[tm - 1:tm, :]
    hc[...] = h_last
    hout_ref[0] = h_last
    out_b = hseq * _silu(proj(OFF_LG, LRU_WIDTH))

    mq = proj(OFF_MQ, MX_WIDTH)
    qm = (mq * _group_rms_scale(mq, ones_bd, MX_HEAD_DIM) * gmq_ref[...]
          * (MX_HEAD_DIM ** -0.5 * LOG2E)).astype(BF16)
    mk_t = mk_ref[0].astype(BF16)
    mv_t = mv_ref[0].astype(BF16)
    lane_q = lax.broadcasted_iota(jnp.int32, (tm, LANES), 1)
    feat_v = lax.broadcasted_iota(jnp.int32, (LANES, N_MEM), 0)
    slabs = []
    for sl in range(MX_WIDTH // LANES):
        qs = qm[:, sl * LANES:(sl + 1) * LANES]
        ks_t = mk_t[sl * LANES:(sl + 1) * LANES, :]
        vs_t = mv_t[sl * LANES:(sl + 1) * LANES, :]
        acc = None
        rl = None
        for half in range(2):
            sel_q = (lane_q >= MX_HEAD_DIM) if half else (lane_q < MX_HEAD_DIM)
            sel_v = (feat_v >= MX_HEAD_DIM) if half else (feat_v < MX_HEAD_DIM)
            s = _dot(jnp.where(sel_q, qs, jnp.zeros_like(qs)), ks_t)
            p = jnp.exp2(s - jnp.max(s, axis=-1, keepdims=True))
            rsum = 1.0 / jnp.sum(p, axis=-1, keepdims=True)
            o = _dot_nt(p.astype(BF16), jnp.where(sel_v, vs_t, jnp.zeros_like(vs_t)))
            acc = o if acc is None else acc + o
            rl = rsum if rl is None else jnp.where(sel_q, rsum, rl)
        slabs.append(acc * rl)
    om = jnp.concatenate(slabs, axis=1)
    out_c = om * _silu(proj(OFF_MG, MX_WIDTH))
    mix_ref[0] = jnp.concatenate([out_b, out_c], axis=1).astype(BF16)


def _in_proj(x, cbuf, h0, mk, mv, wts, rope_tabs, tm, emit_attn_layouts):
    b, t, _ = x.shape
    nt = t // tm
    full = lambda shape: pl.BlockSpec(shape, lambda i, j: (0,) * len(shape))
    per_b = lambda shape: pl.BlockSpec(shape, lambda i, j: (i,) + (0,) * (len(shape) - 1))
    tile = lambda width: pl.BlockSpec((1, tm, width), lambda i, j: (i, j, 0))
    rope_spec = pl.BlockSpec((tm, LANES), lambda i, j: (j, 0))
    in_specs = [
        tile(D_MODEL), per_b((1, CONV_W - 1, LRU_WIDTH)), per_b((1, 1, LRU_WIDTH)),
        full((D_MODEL, IN_WIDTH)), full((1, D_MODEL)), full((1, DA_QK)), full((1, DA_QK)),
        full((1, MX_WIDTH)), full((MXU_TILE, MXU_TILE)),
        rope_spec, rope_spec, rope_spec,
        full((CONV_W, LRU_WIDTH)), full((1, LRU_WIDTH)),
        full((LRU_WIDTH, LRU_WIDTH)), full((1, LRU_WIDTH)),
        full((LRU_WIDTH, LRU_WIDTH)), full((1, LRU_WIDTH)), full((1, LRU_WIDTH)),
        per_b((1, MX_WIDTH, N_MEM)), per_b((1, MX_WIDTH, N_MEM)),
    ]
    out_specs = [tile(DA_QK), tile(DA_QK),
                 pl.BlockSpec((1, tm * DA_HEADS, DA_VDIM), lambda i, j: (i, j, 0)),
                 tile(DA_WIDTH), tile(DA_WIDTH),
                 per_b((1, CONV_W - 1, LRU_WIDTH)), per_b((1, 1, LRU_WIDTH))]
    out_shape = [jax.ShapeDtypeStruct((b, t, DA_QK), BF16),
                 jax.ShapeDtypeStruct((b, t, DA_QK), F32),
                 jax.ShapeDtypeStruct((b, t * DA_HEADS, DA_VDIM), F32),
                 jax.ShapeDtypeStruct((b, t, DA_WIDTH), BF16),
                 jax.ShapeDtypeStruct((b, t, DA_WIDTH), BF16),
                 jax.ShapeDtypeStruct((b, CONV_W - 1, LRU_WIDTH), F32),
                 jax.ShapeDtypeStruct((b, 1, LRU_WIDTH), F32)]
    if emit_attn_layouts:
        out_specs += [tile(DA_QK),
                      pl.BlockSpec((1, DA_HEADS, 1, DA_VDIM + VT_ONES, tm), lambda i, j: (i, 0, j, 0, 0))]
        out_shape += [jax.ShapeDtypeStruct((b, t, DA_QK), BF16),
                      jax.ShapeDtypeStruct((b, DA_HEADS, nt, DA_VDIM + VT_ONES, tm), BF16)]
    return pl.pallas_call(
        functools.partial(_in_proj_kernel, emit_attn_layouts, tm),
        grid=(b, nt),
        in_specs=in_specs,
        out_specs=out_specs,
        out_shape=out_shape,
        scratch_shapes=[pltpu.VMEM((CONV_PAD + tm, LRU_WIDTH), F32), pltpu.VMEM((1, LRU_WIDTH), F32)],
        compiler_params=pltpu.CompilerParams(
            dimension_semantics=("arbitrary", "arbitrary"), vmem_limit_bytes=VMEM_LIMIT_BYTES),
        name="in_proj_prompt" if emit_attn_layouts else "in_proj_sample",
    )(x, cbuf, h0, wts["w_in"], wts["norm_g"], wts["gq"], wts["gk"], wts["gmq"], wts["ones_bd"],
      *rope_tabs, wts["conv_w"], wts["conv_b"], wts["wa"], wts["ba"], wts["wx"], wts["bx"],
      wts["lru_lambda"], mk, mv)


def _lambda_value(lam_ref):
    lp = lam_ref[...]
    s1 = jnp.sum(lp[0:1] * lp[1:2], axis=-1, keepdims=True)
    s2 = jnp.sum(lp[2:3] * lp[3:4], axis=-1, keepdims=True)
    return jnp.exp(s1) - jnp.exp(s2) + LAMBDA_INIT


def _attn_prompt_kernel(tq, tk, hb, q_ref, qn_ref, k_ref, feat_ref, vt_ref, gd_ref, sg_ref, lam_ref,
                        x_ref, mix_ref, wo_ref, y_ref, w_scr, s_scr, tmax_scr, m_scr, acc_scr):
    qi = pl.program_id(2)
    cw = 2 * tq
    lane = lax.broadcasted_iota(jnp.int32, (tq, LANES), 1)
    row = lax.broadcasted_iota(jnp.int32, (tq, LANES), 0)

    def build_w(slot, src_ref, q_tile):
        qfeat = jnp.where(lane > ((q_tile * tq + row) >> 6), NEG_BIG, 0.0).astype(BF16)
        for h in range(hb):
            q = src_ref[0, :, h * LANES:(h + 1) * LANES]
            zero = jnp.zeros_like(q)
            w_scr[slot, h] = jnp.concatenate(
                [jnp.concatenate([jnp.where(lane < DA_HEAD_DIM, q, zero), qfeat], axis=1),
                 jnp.concatenate([jnp.where(lane >= DA_HEAD_DIM, q, zero), qfeat], axis=1)], axis=0)

    cur = qi % 2
    nxt = 1 - cur
    n = ((qi + 1) * tq + tk - 1) // tk

    def key_rows(j, h):
        off = pl.multiple_of(j * tk, tk)
        return jnp.concatenate([k_ref[0, pl.ds(off, tk), h * LANES:(h + 1) * LANES],
                                feat_ref[pl.ds(off, tk), :]], axis=1)

    def stage_scores(kx, slot, h, cc):
        cols = slice(h * cw + cc * MXU_TILE, h * cw + (cc + 1) * MXU_TILE)
        s = _dot_nt(kx, w_scr[slot, h, cc * MXU_TILE:(cc + 1) * MXU_TILE, :])
        s_scr[:, cols] = s
        tmax_scr[:, cols] = jnp.max(s, axis=0, keepdims=True)

    @pl.when(qi == 0)
    def _():
        build_w(0, q_ref, qi)
        for h in range(hb):
            kx = key_rows(0, h)
            for cc in range(cw // MXU_TILE):
                stage_scores(kx, 0, h, cc)

    build_w(nxt, qn_ref, qi + 1)

    m_scr[...] = jnp.full(m_scr.shape, NEG_BIG, F32)

    @pl.when((pl.program_id(0) == 0) & (pl.program_id(1) == 0) & (qi == 0))
    def _():
        acc_scr[...] = jnp.zeros(acc_scr.shape, F32)

    def body(j, c):
        m_old = m_scr[...]
        m_new = jnp.maximum(m_old, tmax_scr[...])
        alpha = jnp.exp2(m_old - m_new)
        m_scr[...] = m_new
        last = j + 1 >= n
        j_next = jnp.where(last, 0, j + 1)
        slot = jnp.where(last, nxt, cur)
        for h in range(hb):
            kx = key_rows(j_next, h)
            for cc in range(cw // MXU_TILE):
                cols = slice(h * cw + cc * MXU_TILE, h * cw + (cc + 1) * MXU_TILE)
                acc_c = alpha[:, cols] * acc_scr[:, cols]
                for r in range(tk // MXU_TILE):
                    rows = slice(r * MXU_TILE, (r + 1) * MXU_TILE)
                    p = jnp.exp2(s_scr[rows, cols] - m_new[:, cols]).astype(BF16)
                    acc_c = acc_c + _dot(vt_ref[0, h, j, :, rows], p)
                acc_scr[:, cols] = acc_c
                stage_scores(kx, slot, h, cc)
        return c

    lax.fori_loop(0, n, body, 0)

    y_ref[0] = x_ref[0] + _dot(mix_ref[0], wo_ref[DA_WIDTH:, :])
    acc = acc_scr[...]
    acc = acc[:DA_VDIM] * (1.0 / acc[DA_VDIM:DA_VDIM + 1])
    lam = _lambda_value(lam_ref)
    heads = []
    for h in range(hb):
        ot = acc[:, h * cw:h * cw + tq] - lam * acc[:, h * cw + tq:(h + 1) * cw]
        ot = ot * lax.rsqrt(jnp.mean(ot * ot, axis=0, keepdims=True) + EPS)
        o = ot.T * (sg_ref[...] * (1.0 - LAMBDA_INIT))
        heads.append((o * gd_ref[0, :, h * LANES:(h + 1) * LANES].astype(F32)).astype(BF16))
    out_a = jnp.concatenate(heads, axis=1)
    y_ref[0] += _dot(out_a, wo_ref[:DA_WIDTH, :])


def _attn_prompt(q, kb, vt, gd, subln_g, lam_params, x, mix, w_out, tq, tk, hb):
    b, t, _ = q.shape
    assert hb == DA_HEADS, "the fused output projection needs every head of a query tile in one step"
    nk, nq = t // tk, t // tq
    assert t % CHUNK == 0 and t // CHUNK <= LANES, "chunk one-hot must fit one 128-lane slab"
    assert DA_HEADS % hb == 0
    key_chunk = np.arange(t) // CHUNK
    feat = jnp.asarray(key_chunk[:, None] == np.arange(LANES)[None, :], BF16)
    hw = hb * LANES
    vrows = DA_VDIM + VT_ONES
    once = dict(pipeline_mode=pl.Buffered(1))
    return pl.pallas_call(
        functools.partial(_attn_prompt_kernel, tq, tk, hb),
        grid=(b, DA_HEADS // hb, nq),
        in_specs=[
            pl.BlockSpec((1, tq, hw), lambda i, h, j: (i, j, h)),
            pl.BlockSpec((1, tq, hw), lambda i, h, j: (i, jnp.minimum(j + 1, nq - 1), h)),
            pl.BlockSpec((1, t, hw), lambda i, h, j: (i, 0, h), **once),
            pl.BlockSpec((t, LANES), lambda i, h, j: (0, 0), **once),
            pl.BlockSpec((1, hb, nk, vrows, tk), lambda i, h, j: (i, h, 0, 0, 0), **once),
            pl.BlockSpec((1, tq, hw), lambda i, h, j: (i, j, h)),
            pl.BlockSpec((1, DA_VDIM), lambda i, h, j: (0, 0)),
            pl.BlockSpec((4, DA_HEAD_DIM), lambda i, h, j: (0, 0)),
            pl.BlockSpec((1, tq, D_MODEL), lambda i, h, j: (i, j, 0)),
            pl.BlockSpec((1, tq, DA_WIDTH), lambda i, h, j: (i, j, 0)),
            pl.BlockSpec((D_MODEL, D_MODEL), lambda i, h, j: (0, 0), **once),
        ],
        out_specs=pl.BlockSpec((1, tq, D_MODEL), lambda i, h, j: (i, j, 0)),
        out_shape=jax.ShapeDtypeStruct((b, t, D_MODEL), F32),
        scratch_shapes=[pltpu.VMEM((2, hb, 2 * tq, 2 * LANES), BF16),
                        pltpu.VMEM((tk, hb * 2 * tq), F32),
                        pltpu.VMEM((1, hb * 2 * tq), F32), pltpu.VMEM((1, hb * 2 * tq), F32),
                        pltpu.VMEM((vrows, hb * 2 * tq), F32)],
        compiler_params=pltpu.CompilerParams(
            dimension_semantics=("arbitrary", "arbitrary", "arbitrary"),
            vmem_limit_bytes=VMEM_LIMIT_BYTES),
        name="diff_attn_prompt",
    )(q, q, kb, feat, vt, gd, subln_g, lam_params, x, mix, w_out)


def _attn_sample_kernel(past, q_ref, kn_ref, vn_ref, kp_ref, vp_ref, gd_ref, sg_ref, lam_ref, o_ref):
    ts = q_ref.shape[1]
    lam = _lambda_value(lam_ref)
    lane = lax.broadcasted_iota(jnp.int32, (ts, LANES), 1)
    qrow = past + lax.broadcasted_iota(jnp.int32, (ts, past), 0)
    vis_p = (lax.broadcasted_iota(jnp.int32, (ts, past), 1) >> 6) <= (qrow >> 6)
    qrow_n = past + lax.broadcasted_iota(jnp.int32, (ts, ts), 0)
    vis_n = ((past + lax.broadcasted_iota(jnp.int32, (ts, ts), 1)) >> 6) <= (qrow_n >> 6)
    for h in range(DA_HEADS):
        sl = slice(h * LANES, (h + 1) * LANES)
        q = q_ref[0, :, sl]
        kp_t = kp_ref[0, sl, :].astype(BF16)
        kn = kn_ref[0, :, sl].astype(BF16)
        zero = jnp.zeros_like(q)
        wp, wn = [], []
        for c in range(2):
            qc = jnp.where((lane >= DA_HEAD_DIM) if c else (lane < DA_HEAD_DIM), q, zero)
            sp = jnp.where(vis_p, _dot(qc, kp_t), NEG_BIG)
            sn = jnp.where(vis_n, _dot_nt(qc, kn), NEG_BIG)
            m = jnp.maximum(jnp.max(sp, axis=-1, keepdims=True), jnp.max(sn, axis=-1, keepdims=True))
            pp = jnp.exp2(sp - m)
            pn = jnp.exp2(sn - m)
            inv = 1.0 / (jnp.sum(pp, axis=-1, keepdims=True) + jnp.sum(pn, axis=-1, keepdims=True))
            wp.append(pp * inv)
            wn.append(pn * inv)
        wgt_p = (wp[0] - lam * wp[1]).astype(BF16)
        wgt_n = (wn[0] - lam * wn[1]).astype(BF16)
        v_past = vp_ref[0, pl.ds(h, past, stride=DA_HEADS), :].astype(BF16)
        v_new = vn_ref[0, pl.ds(h, ts, stride=DA_HEADS), :].astype(BF16)
        o = _dot(wgt_p, v_past) + _dot(wgt_n, v_new)
        o = o * lax.rsqrt(jnp.mean(o * o, axis=-1, keepdims=True) + EPS)
        o = o * (sg_ref[...] * (1.0 - LAMBDA_INIT))
        o_ref[0, :, sl] = (o * gd_ref[0, :, sl].astype(F32)).astype(BF16)


def _attn_sample(q, k_new, v_new, k_past_t, v_past, gd, subln_g, lam_params):
    b, ts, _ = q.shape
    past = k_past_t.shape[2]
    new = pl.BlockSpec((1, ts, DA_WIDTH), lambda i: (i, 0, 0))
    new_v = pl.BlockSpec((1, ts * DA_HEADS, DA_VDIM), lambda i: (i, 0, 0))
    old_kt = pl.BlockSpec((1, DA_QK, past), lambda i: (i, 0, 0))
    old_v = pl.BlockSpec((1, past * DA_HEADS, DA_VDIM), lambda i: (i, 0, 0))
    return pl.pallas_call(
        functools.partial(_attn_sample_kernel, past),
        grid=(b,),
        in_specs=[new, new, new_v, old_kt, old_v, new,
                  pl.BlockSpec((1, DA_VDIM), lambda i: (0, 0)),
                  pl.BlockSpec((4, DA_HEAD_DIM), lambda i: (0, 0))],
        out_specs=new,
        out_shape=jax.ShapeDtypeStruct((b, ts, DA_WIDTH), BF16),
        compiler_params=pltpu.CompilerParams(
            dimension_semantics=("arbitrary",), vmem_limit_bytes=VMEM_LIMIT_BYTES),
        name="diff_attn_sample",
    )(q, k_new, v_new, k_past_t, v_past, gd, subln_g, lam_params)


def _out_proj_kernel(x_ref, a_ref, mix_ref, w_ref, y_ref):
    y_ref[0] = (x_ref[0] + _dot(a_ref[0], w_ref[:DA_WIDTH, :]) + _dot(mix_ref[0], w_ref[DA_WIDTH:, :]))


def _out_proj(x, out_a, mix, w_out, tm):
    b, t, _ = x.shape
    tile = lambda width: pl.BlockSpec((1, tm, width), lambda i, j: (i, j, 0))
    return pl.pallas_call(
        _out_proj_kernel,
        grid=(b, t // tm),
        in_specs=[tile(D_MODEL), tile(DA_WIDTH), tile(DA_WIDTH),
                  pl.BlockSpec((D_MODEL, D_MODEL), lambda i, j: (0, 0))],
        out_specs=tile(D_MODEL),
        out_shape=jax.ShapeDtypeStruct((b, t, D_MODEL), F32),
        compiler_params=pltpu.CompilerParams(
            dimension_semantics=("arbitrary", "arbitrary"), vmem_limit_bytes=VMEM_LIMIT_BYTES),
        name="out_proj",
    )(x, out_a, mix, w_out)


def _rope_tables(pos):
    half = ROT_DIM // 2
    inv = ROPE_THETA ** (-jnp.arange(0, ROT_DIM, 2, dtype=F32) / ROT_DIM)
    ang = pos.astype(F32)[:, None] * inv[None, :]
    cos, sin = jnp.cos(ang), jnp.sin(ang)
    t = pos.shape[0]
    ones = jnp.ones((t, DA_HEAD_DIM - ROT_DIM), F32)
    zeros = jnp.zeros((t, DA_HEAD_DIM - ROT_DIM), F32)
    zh = jnp.zeros((t, half), F32)
    c = jnp.concatenate([cos, cos, ones], axis=1)
    s_up = jnp.concatenate([-sin, zh, zeros], axis=1)
    s_dn = jnp.concatenate([zh, sin, zeros], axis=1)
    rep = LANES // DA_HEAD_DIM
    return tuple(jnp.tile(a, (1, rep)) for a in (c, s_up, s_dn))


def _block_diag(w):
    n, bi, bj = w.shape
    eye = jnp.eye(n, dtype=w.dtype)
    return (eye[:, None, :, None] * w[:, :, None, :]).reshape(n * bi, n * bj)


def _pick_tile(t, pref):
    return pref if t % pref == 0 else t


def kernel(x_prompt, x_sample, mem_prompt, cache_diff_k, cache_diff_v, cache_mem_k, cache_mem_v,
           state_lru_conv, state_lru_h, norm_g, w_in, da_q_norm_g, da_k_norm_g, lambda_q1, lambda_k1,
           lambda_q2, lambda_k2, da_subln_g, lru_conv_w, lru_conv_b, lru_w_a, lru_b_a, lru_w_x, lru_b_x,
           lru_lambda, mem_norm_g, w_mem_kv, mx_q_norm_g, mx_k_norm_g, w_out):
    depth = w_in.shape[0]
    assert depth == 1
    bp, tp, _ = x_prompt.shape
    bs, ts, _ = x_sample.shape
    past = cache_diff_k.shape[2]
    l = 0
    grp = np.arange(MXU_TILE) // DA_HEAD_DIM
    ones_bd = jnp.asarray(grp[:, None] == grp[None, :], BF16)
    wts = dict(
        w_in=w_in[l].astype(BF16), norm_g=norm_g[l][None, :],
        gq=jnp.tile(da_q_norm_g[l], DA_QK // DA_HEAD_DIM)[None, :],
        gk=jnp.tile(da_k_norm_g[l], DA_QK // DA_HEAD_DIM)[None, :],
        gmq=jnp.tile(mx_q_norm_g[l], MX_HEADS)[None, :], ones_bd=ones_bd,
        conv_w=lru_conv_w[l], conv_b=lru_conv_b[l][None, :],
        wa=_block_diag(lru_w_a[l]).astype(BF16), ba=lru_b_a[l][None, :],
        wx=_block_diag(lru_w_x[l]).astype(BF16), bx=lru_b_x[l][None, :],
        lru_lambda=lru_lambda[l][None, :])
    lam_params = jnp.stack([lambda_q1[l], lambda_k1[l], lambda_q2[l], lambda_k2[l]])
    subln_g = da_subln_g[l][None, :]
    w_out_b = w_out[l].astype(BF16)

    mk_p, mv_p = _mem_kv(mem_prompt, mem_norm_g[l][None, :], w_mem_kv[l].astype(BF16),
                         jnp.tile(mx_k_norm_g[l], MX_HEADS)[None, :], ones_bd)
    tm_p = _pick_tile(tp, 512)
    tq_p = _pick_tile(tp, 512)
    pos_p = jnp.arange(tp, dtype=jnp.int32)
    q_p, k_p, v_p, gd_p, mix_p, c_p, h_p, kb_p, vt_p = _in_proj(
        x_prompt, jnp.zeros((bp, CONV_W - 1, LRU_WIDTH), F32), jnp.zeros((bp, 1, LRU_WIDTH), F32),
        mk_p, mv_p, wts, _rope_tables(pos_p), tm_p, True)
    y_p = _attn_prompt(q_p, kb_p, vt_p, gd_p, subln_g, lam_params, x_prompt, mix_p, w_out_b,
                       tq_p, tm_p, ATTN_HEADS_PER_STEP)

    pos_s = past + jnp.arange(ts, dtype=jnp.int32)
    q_s, k_s, v_s, gd_s, mix_s, c_s, h_s = _in_proj(
        x_sample, state_lru_conv[l], state_lru_h[l][:, None, :],
        jnp.swapaxes(cache_mem_k[l].reshape(bs, N_MEM, MX_WIDTH), 1, 2),
        jnp.swapaxes(cache_mem_v[l].reshape(bs, N_MEM, MX_WIDTH), 1, 2),
        wts, _rope_tables(pos_s), ts, False)
    oa_s = _attn_sample(q_s, k_s, v_s, jnp.swapaxes(cache_diff_k[l].reshape(bs, past, DA_QK), 1, 2),
                        cache_diff_v[l].reshape(bs, past * DA_HEADS, DA_VDIM), gd_s, subln_g, lam_params)
    y_s = _out_proj(x_sample, oa_s, mix_s, w_out_b, ts)

    return (y_p, y_s,
            k_p.reshape(1, bp, tp, DA_HEADS, 2, DA_HEAD_DIM), v_p.reshape(1, bp, tp, DA_HEADS, DA_VDIM),
            jnp.swapaxes(mk_p, 1, 2).reshape(1, bp, N_MEM, MX_HEADS, MX_HEAD_DIM),
            jnp.swapaxes(mv_p, 1, 2).reshape(1, bp, N_MEM, MX_HEADS, MX_HEAD_DIM),
            c_p[None], h_p.reshape(1, bp, LRU_WIDTH),
            k_s.reshape(1, bs, ts, DA_HEADS, 2, DA_HEAD_DIM), v_s.reshape(1, bs, ts, DA_HEADS, DA_VDIM),
            c_s[None], h_s.reshape(1, bs, LRU_WIDTH))
```

```python
import functools
import math

import numpy as np
import jax
import jax.numpy as jnp
from jax import lax
from jax.experimental import pallas as pl
from jax.experimental.pallas import tpu as pltpu

D_MODEL = 1024
CHUNK = 64
EPS = 1e-6
DA_HEADS = 4
DA_HEAD_DIM = 64
DA_VDIM = 128
DA_QK = 512
DA_WIDTH = 512
ROPE_THETA = 500000.0
ROT_DIM = 16
LRU_WIDTH = 256
LRU_BLOCKS = 4
CONV_W = 4
LRU_C = 8.0
MX_HEADS = 4
MX_HEAD_DIM = 64
MX_WIDTH = 256
N_MEM = 256
LAMBDA_INIT = 0.8 - 0.6 * math.exp(-0.3 * 0)
NEG_BIG = -1e30
LOG2E = math.log2(math.e)

OFF_DQ, OFF_DK, OFF_DV, OFF_DG = 0, 512, 1024, 1536
OFF_LX, OFF_LG, OFF_MQ, OFF_MG = 2048, 2304, 2560, 2816
IN_WIDTH = 3072

VMEM_LIMIT_BYTES = 58 * 1024 * 1024
LANES = 128
MXU_TILE = 256
CONV_PAD = 8
ATTN_HEADS_PER_STEP = 4
SCAN_CHUNK = 32
VT_ONES = 16

F32 = jnp.float32
BF16 = jnp.bfloat16


def _dot(a, b):
    return jnp.dot(a, b, preferred_element_type=F32)


def _dot_nt(a, b):
    return lax.dot_general(a, b, (((1,), (1,)), ((), ())), preferred_element_type=F32)


def _sigmoid(x):
    return 0.5 + 0.5 * jnp.tanh(0.5 * x)


def _silu(x):
    return x * _sigmoid(x)


def _group_rms_scale(x, ones_bd, group):
    xsq = (x * x).astype(BF16)
    wb = ones_bd.shape[0]
    ss = jnp.concatenate([_dot(xsq[:, c:c + wb], ones_bd) for c in range(0, x.shape[1], wb)], axis=1)
    return lax.rsqrt(ss * (1.0 / group) + EPS)


def _rope(x, c, s_up, s_dn):
    w = x.shape[1]
    return x * c + pltpu.roll(x, w - ROT_DIM // 2, 1) * s_up + pltpu.roll(x, ROT_DIM // 2, 1) * s_dn


def _mem_kv_kernel(mem_ref, g_ref, w_ref, gk_ref, ones_ref, k_ref, v_ref):
    x = mem_ref[0]
    y = x * lax.rsqrt(jnp.mean(x * x, axis=-1, keepdims=True) + EPS) * g_ref[...]
    kv = _dot(y.astype(BF16), w_ref[...])
    k = kv[:, :MX_WIDTH]
    k_ref[0] = (k * _group_rms_scale(k, ones_ref[...], MX_HEAD_DIM) * gk_ref[...]).T
    v_ref[0] = kv[:, MX_WIDTH:].T


def _mem_kv(mem, mem_norm_g, w_mem_kv, gk_tiled, ones256):
    b, n, _ = mem.shape
    full = lambda shape: pl.BlockSpec(shape, lambda i: (0,) * len(shape))
    return pl.pallas_call(
        _mem_kv_kernel,
        grid=(b,),
        in_specs=[
            pl.BlockSpec((1, n, D_MODEL), lambda i: (i, 0, 0)),
            full((1, D_MODEL)),
            full((D_MODEL, 2 * MX_WIDTH)),
            full((1, MX_WIDTH)),
            full((MX_WIDTH, MX_WIDTH)),
        ],
        out_specs=[pl.BlockSpec((1, MX_WIDTH, n), lambda i: (i, 0, 0))] * 2,
        out_shape=[jax.ShapeDtypeStruct((b, MX_WIDTH, n), F32)] * 2,
        compiler_params=pltpu.CompilerParams(
            dimension_semantics=("arbitrary",), vmem_limit_bytes=VMEM_LIMIT_BYTES),
        name="mem_kv",
    )(mem, mem_norm_g, w_mem_kv, gk_tiled, ones256)


def _lru_scan(a, b, h_prev):
    tm = a.shape[0]
    chunk = min(tm, SCAN_CHUNK)
    row = lax.broadcasted_iota(jnp.int32, (chunk, a.shape[1]), 0)
    out = []
    for c0 in range(0, tm, chunk):
        ac, bc = a[c0:c0 + chunk], b[c0:c0 + chunk]
        d = 1
        while d < chunk:
            keep = row >= d
            bc = ac * jnp.where(keep, pltpu.roll(bc, d, 0), 0.0) + bc
            ac = ac * jnp.where(keep, pltpu.roll(ac, d, 0), 1.0)
            d *= 2
        hc = bc + ac * h_prev
        h_prev = hc[chunk - 1:chunk, :]
        out.append(hc)
    return jnp.concatenate(out, axis=0)


def _in_proj_kernel(emit_attn_layouts, tm,
                    x_ref, cbuf_ref, h0_ref, w_ref, ng_ref, gq_ref, gk_ref, gmq_ref, ones_ref,
                    rc_ref, ru_ref, rd_ref, cw_ref, cb_ref, wa_ref, ba_ref, wx_ref, bx_ref,
                    lam_ref, mk_ref, mv_ref, *rest):
    if emit_attn_layouts:
        (q_ref, k_ref, v_ref, gd_ref, mix_ref, cout_ref, hout_ref, kb_ref, vt_ref,
         xbuf, hc) = rest
    else:
        q_ref, k_ref, v_ref, gd_ref, mix_ref, cout_ref, hout_ref, xbuf, hc = rest
    t = pl.program_id(1)

    @pl.when(t == 0)
    def _():
        xbuf[pl.ds(CONV_PAD - (CONV_W - 1), CONV_W - 1), :] = cbuf_ref[0]
        hc[...] = h0_ref[0]

    x = x_ref[0]
    hn = (x * lax.rsqrt(jnp.mean(x * x, axis=-1, keepdims=True) + EPS) * ng_ref[...]).astype(BF16)

    def proj(off, width):
        return _dot(hn, w_ref[:, off:off + width])

    rc = jnp.concatenate([rc_ref[...]] * (DA_QK // LANES), axis=1)
    ru = jnp.concatenate([ru_ref[...]] * (DA_QK // LANES), axis=1)
    rd = jnp.concatenate([rd_ref[...]] * (DA_QK // LANES), axis=1)
    ones_bd = ones_ref[...]

    dq = proj(OFF_DQ, DA_QK)
    q = _rope(dq * _group_rms_scale(dq, ones_bd, DA_HEAD_DIM) * gq_ref[...], rc, ru, rd)
    q_ref[0] = (q * (DA_HEAD_DIM ** -0.5 * LOG2E)).astype(BF16)
    dk = proj(OFF_DK, DA_QK)
    k = _rope(dk * _group_rms_scale(dk, ones_bd, DA_HEAD_DIM) * gk_ref[...], rc, ru, rd)
    k_ref[0] = k
    dv = proj(OFF_DV, DA_WIDTH)
    for h in range(DA_HEADS):
        v_ref[0, pl.ds(h, tm, stride=DA_HEADS), :] = dv[:, h * DA_VDIM:(h + 1) * DA_VDIM]
    if emit_attn_layouts:
        kb_ref[0] = k.astype(BF16)
        for h in range(DA_HEADS):
            vt_ref[0, h, 0, :DA_VDIM, :] = dv[:, h * DA_VDIM:(h + 1) * DA_VDIM].T.astype(BF16)
            vt_ref[0, h, 0, DA_VDIM:, :] = jnp.ones((VT_ONES, tm), BF16)
    gd_ref[0] = _silu(proj(OFF_DG, DA_WIDTH)).astype(BF16)

    lx = proj(OFF_LX, LRU_WIDTH)
    xbuf[pl.ds(CONV_PAD, tm), :] = lx
    xc = cb_ref[...]
    for j in range(CONV_W):
        xc = xc + xbuf[pl.ds(CONV_PAD - (CONV_W - 1) + j, tm), :] * cw_ref[j:j + 1, :]
    tail = xbuf[pl.ds(CONV_PAD + tm - (CONV_W - 1), CONV_W - 1), :]
    cout_ref[0] = tail
    xbuf[pl.ds(CONV_PAD - (CONV_W - 1), CONV_W - 1), :] = tail
    xcb = xc.astype(BF16)
    r = _sigmoid(_dot(xcb, wa_ref[...]) + ba_ref[...])
    i = _sigmoid(_dot(xcb, wx_ref[...]) + bx_ref[...])
    neg_lam = -lam_ref[...]
    softplus = jnp.maximum(neg_lam, 0.0) + jnp.log1p(jnp.exp(-jnp.abs(neg_lam)))
    log_a = (-LRU_C) * r * softplus
    a = jnp.exp(log_a)
    bb = (1.0 + a) * jnp.sqrt(-jnp.tanh(0.5 * log_a)) * (i * xc)
    hseq = _lru_scan(a, bb, hc[...])
    h_last = hseq[tm - 1:tm, :]
    hc[...] = h_last
    hout_ref[0] = h_last
    out_b = hseq * _silu(proj(OFF_LG, LRU_WIDTH))

    mq = proj(OFF_MQ, MX_WIDTH)
    qm = (mq * _group_rms_scale(mq, ones_bd, MX_HEAD_DIM) * gmq_ref[...]
          * (MX_HEAD_DIM ** -0.5 * LOG2E)).astype(BF16)
    mk_t = mk_ref[0].astype(BF16)
    mv_t = mv_ref[0].astype(BF16)
    lane_q = lax.broadcasted_iota(jnp.int32, (tm, LANES), 1)
    feat_v = lax.broadcasted_iota(jnp.int32, (LANES, N_MEM), 0)
    slabs = []
    for sl in range(MX_WIDTH // LANES):
        qs = qm[:, sl * LANES:(sl + 1) * LANES]
        ks_t = mk_t[sl * LANES:(sl + 1) * LANES, :]
        vs_t = mv_t[sl * LANES:(sl + 1) * LANES, :]
        acc = None
        rl = None
        for half in range(2):
            sel_q = (lane_q >= MX_HEAD_DIM) if half else (lane_q < MX_HEAD_DIM)
            sel_v = (feat_v >= MX_HEAD_DIM) if half else (feat_v < MX_HEAD_DIM)
            s = _dot(jnp.where(sel_q, qs, jnp.zeros_like(qs)), ks_t)
            p = jnp.exp2(s - jnp.max(s, axis=-1, keepdims=True))
            rsum = 1.0 / jnp.sum(p, axis=-1, keepdims=True)
            o = _dot_nt(p.astype(BF16), jnp.where(sel_v, vs_t, jnp.zeros_like(vs_t)))
            acc = o if acc is None else acc + o
            rl = rsum if rl is None else jnp.where(sel_q, rsum, rl)
        slabs.append(acc * rl)
    om = jnp.concatenate(slabs, axis=1)
    out_c = om * _silu(proj(OFF_MG, MX_WIDTH))
    mix_ref[0] = jnp.concatenate([out_b, out_c], axis=1).astype(BF16)


def _in_proj(x, cbuf, h0, mk, mv, wts, rope_tabs, tm, emit_attn_layouts):
    b, t, _ = x.shape
    nt = t // tm
    full = lambda shape: pl.BlockSpec(shape, lambda i, j: (0,) * len(shape))
    per_b = lambda shape: pl.BlockSpec(shape, lambda i, j: (i,) + (0,) * (len(shape) - 1))
    tile = lambda width: pl.BlockSpec((1, tm, width), lambda i, j: (i, j, 0))
    rope_spec = pl.BlockSpec((tm, LANES), lambda i, j: (j, 0))
    in_specs = [
        tile(D_MODEL), per_b((1, CONV_W - 1, LRU_WIDTH)), per_b((1, 1, LRU_WIDTH)),
        full((D_MODEL, IN_WIDTH)), full((1, D_MODEL)), full((1, DA_QK)), full((1, DA_QK)),
        full((1, MX_WIDTH)), full((MXU_TILE, MXU_TILE)),
        rope_spec, rope_spec, rope_spec,
        full((CONV_W, LRU_WIDTH)), full((1, LRU_WIDTH)),
        full((LRU_WIDTH, LRU_WIDTH)), full((1, LRU_WIDTH)),
        full((LRU_WIDTH, LRU_WIDTH)), full((1, LRU_WIDTH)), full((1, LRU_WIDTH)),
        per_b((1, MX_WIDTH, N_MEM)), per_b((1, MX_WIDTH, N_MEM)),
    ]
    out_specs = [tile(DA_QK), tile(DA_QK),
                 pl.BlockSpec((1, tm * DA_HEADS, DA_VDIM), lambda i, j: (i, j, 0)),
                 tile(DA_WIDTH), tile(DA_WIDTH),
                 per_b((1, CONV_W - 1, LRU_WIDTH)), per_b((1, 1, LRU_WIDTH))]
    out_shape = [jax.ShapeDtypeStruct((b, t, DA_QK), BF16),
                 jax.ShapeDtypeStruct((b, t, DA_QK), F32),
                 jax.ShapeDtypeStruct((b, t * DA_HEADS, DA_VDIM), F32),
                 jax.ShapeDtypeStruct((b, t, DA_WIDTH), BF16),
                 jax.ShapeDtypeStruct((b, t, DA_WIDTH), BF16),
                 jax.ShapeDtypeStruct((b, CONV_W - 1, LRU_WIDTH), F32),
                 jax.ShapeDtypeStruct((b, 1, LRU_WIDTH), F32)]
    if emit_attn_layouts:
        out_specs += [tile(DA_QK),
                      pl.BlockSpec((1, DA_HEADS, 1, DA_VDIM + VT_ONES, tm), lambda i, j: (i, 0, j, 0, 0))]
        out_shape += [jax.ShapeDtypeStruct((b, t, DA_QK), BF16),
                      jax.ShapeDtypeStruct((b, DA_HEADS, nt, DA_VDIM + VT_ONES, tm), BF16)]
    return pl.pallas_call(
        functools.partial(_in_proj_kernel, emit_attn_layouts, tm),
        grid=(b, nt),
        in_specs=in_specs,
        out_specs=out_specs,
        out_shape=out_shape,
        scratch_shapes=[pltpu.VMEM((CONV_PAD + tm, LRU_WIDTH), F32), pltpu.VMEM((1, LRU_WIDTH), F32)],
        compiler_params=pltpu.CompilerParams(
            dimension_semantics=("arbitrary", "arbitrary"), vmem_limit_bytes=VMEM_LIMIT_BYTES),
        name="in_proj_prompt" if emit_attn_layouts else "in_proj_sample",
    )(x, cbuf, h0, wts["w_in"], wts["norm_g"], wts["gq"], wts["gk"], wts["gmq"], wts["ones_bd"],
      *rope_tabs, wts["conv_w"], wts["conv_b"], wts["wa"], wts["ba"], wts["wx"], wts["bx"],
      wts["lru_lambda"], mk, mv)


def _lambda_value(lam_ref):
    lp = lam_ref[...]
    s1 = jnp.sum(lp[0:1] * lp[1:2], axis=-1, keepdims=True)
    s2 = jnp.sum(lp[2:3] * lp[3:4], axis=-1, keepdims=True)
    return jnp.exp(s1) - jnp.exp(s2) + LAMBDA_INIT


def _attn_prompt_kernel(tq, tk, hb, q_ref, qn_ref, k_ref, feat_ref, vt_ref, gd_ref, sg_ref, lam_ref,
                        x_ref, mix_ref, wo_ref, y_ref, w_scr, s_scr, tmax_scr, m_scr, acc_scr):
    qi = pl.program_id(2)
    cw = 2 * tq
    lane = lax.broadcasted_iota(jnp.int32, (tq, LANES), 1)
    row = lax.broadcasted_iota(jnp.int32, (tq, LANES), 0)

    def build_w(slot, src_ref, q_tile):
        qfeat = jnp.where(lane > ((q_tile * tq + row) >> 6), NEG_BIG, 0.0).astype(BF16)
        for h in range(hb):
            q = src_ref[0, :, h * LANES:(h + 1) * LANES]
            zero = jnp.zeros_like(q)
            w_scr[slot, h] = jnp.concatenate(
                [jnp.concatenate([jnp.where(lane < DA_HEAD_DIM, q, zero), qfeat], axis=1),
                 jnp.concatenate([jnp.where(lane >= DA_HEAD_DIM, q, zero), qfeat], axis=1)], axis=0)

    cur = qi % 2
    nxt = 1 - cur
    n = ((qi + 1) * tq + tk - 1) // tk

    def key_rows(j, h):
        off = pl.multiple_of(j * tk, tk)
        return jnp.concatenate([k_ref[0, pl.ds(off, tk), h * LANES:(h + 1) * LANES],
                                feat_ref[pl.ds(off, tk), :]], axis=1)

    def stage_scores(kx, slot, h, cc):
        cols = slice(h * cw + cc * MXU_TILE, h * cw + (cc + 1) * MXU_TILE)
        s = _dot_nt(kx, w_scr[slot, h, cc * MXU_TILE:(cc + 1) * MXU_TILE, :])
        s_scr[:, cols] = s
        tmax_scr[:, cols] = jnp.max(s, axis=0, keepdims=True)

    @pl.when(qi == 0)
    def _():
        build_w(0, q_ref, qi)
        for h in range(hb):
            kx = key_rows(0, h)
            for cc in range(cw // MXU_TILE):
                stage_scores(kx, 0, h, cc)

    build_w(nxt, qn_ref, qi + 1)

    m_scr[...] = jnp.full(m_scr.shape, NEG_BIG, F32)
    acc_scr[...] = jnp.zeros(acc_scr.shape, F32)

    def body(j, c):
        m_old = m_scr[...]
        m_new = jnp.maximum(m_old, tmax_scr[...])
        alpha = jnp.exp2(m_old - m_new)
        m_scr[...] = m_new
        last = j + 1 >= n
        j_next = jnp.where(last, 0, j + 1)
        slot = jnp.where(last, nxt, cur)
        for h in range(hb):
            kx = key_rows(j_next, h)
            for cc in range(cw // MXU_TILE):
                cols = slice(h * cw + cc * MXU_TILE, h * cw + (cc + 1) * MXU_TILE)
                acc_c = alpha[:, cols] * acc_scr[:, cols]
                for r in range(tk // MXU_TILE):
                    rows = slice(r * MXU_TILE, (r + 1) * MXU_TILE)
                    p = jnp.exp2(s_scr[rows, cols] - m_new[:, cols]).astype(BF16)
                    acc_c = acc_c + _dot(vt_ref[0, h, j, :, rows], p)
                acc_scr[:, cols] = acc_c
                stage_scores(kx, slot, h, cc)
        return c

    lax.fori_loop(0, n, body, 0)

    y_ref[0] = x_ref[0] + _dot(mix_ref[0], wo_ref[DA_WIDTH:, :])
    acc = acc_scr[...]
    acc = acc[:DA_VDIM] * (1.0 / acc[DA_VDIM:DA_VDIM + 1])
    lam = _lambda_value(lam_ref)
    heads = []
    for h in range(hb):
        ot = acc[:, h * cw:h * cw + tq] - lam * acc[:, h * cw + tq:(h + 1) * cw]
        ot = ot * lax.rsqrt(jnp.mean(ot * ot, axis=0, keepdims=True) + EPS)
        o = ot.T * (sg_ref[...] * (1.0 - LAMBDA_INIT))
        heads.append((o * gd_ref[0, :, h * LANES:(h + 1) * LANES].astype(F32)).astype(BF16))
    out_a = jnp.concatenate(heads, axis=1)
    y_ref[0] += _dot(out_a, wo_ref[:DA_WIDTH, :])


def _attn_prompt(q, kb, vt, gd, subln_g, lam_params, x, mix, w_out, tq, tk, hb):
    b, t, _ = q.shape
    assert hb == DA_HEADS, "the fused output projection needs every head of a query tile in one step"
    nk, nq = t // tk, t // tq
    assert t % CHUNK == 0 and t // CHUNK <= LANES, "chunk one-hot must fit one 128-lane slab"
    assert DA_HEADS % hb == 0
    key_chunk = np.arange(t) // CHUNK
    feat = jnp.asarray(key_chunk[:, None] == np.arange(LANES)[None, :], BF16)
    hw = hb * LANES
    vrows = DA_VDIM + VT_ONES
    once = dict(pipeline_mode=pl.Buffered(1))
    return pl.pallas_call(
        functools.partial(_attn_prompt_kernel, tq, tk, hb),
        grid=(b, DA_HEADS // hb, nq),
        in_specs=[
            pl.BlockSpec((1, tq, hw), lambda i, h, j: (i, j, h)),
            pl.BlockSpec((1, tq, hw), lambda i, h, j: (i, jnp.minimum(j + 1, nq - 1), h)),
            pl.BlockSpec((1, t, hw), lambda i, h, j: (i, 0, h)),
            pl.BlockSpec((t, LANES), lambda i, h, j: (0, 0), **once),
            pl.BlockSpec((1, hb, nk, vrows, tk), lambda i, h, j: (i, h, 0, 0, 0), **once),
            pl.BlockSpec((1, tq, hw), lambda i, h, j: (i, j, h)),
            pl.BlockSpec((1, DA_VDIM), lambda i, h, j: (0, 0)),
            pl.BlockSpec((4, DA_HEAD_DIM), lambda i, h, j: (0, 0)),
            pl.BlockSpec((1, tq, D_MODEL), lambda i, h, j: (i, j, 0)),
            pl.BlockSpec((1, tq, DA_WIDTH), lambda i, h, j: (i, j, 0)),
            pl.BlockSpec((D_MODEL, D_MODEL), lambda i, h, j: (0, 0), **once),
        ],
        out_specs=pl.BlockSpec((1, tq, D_MODEL), lambda i, h, j: (i, j, 0)),
        out_shape=jax.ShapeDtypeStruct((b, t, D_MODEL), F32),
        scratch_shapes=[pltpu.VMEM((2, hb, 2 * tq, 2 * LANES), BF16),
                        pltpu.VMEM((tk, hb * 2 * tq), F32),
                        pltpu.VMEM((1, hb * 2 * tq), F32), pltpu.VMEM((1, hb * 2 * tq), F32),
                        pltpu.VMEM((vrows, hb * 2 * tq), F32)],
        compiler_params=pltpu.CompilerParams(
            dimension_semantics=("arbitrary", "arbitrary", "arbitrary"),
            vmem_limit_bytes=VMEM_LIMIT_BYTES),
        name="diff_attn_prompt",
    )(q, q, kb, feat, vt, gd, subln_g, lam_params, x, mix, w_out)


def _attn_sample_kernel(past, q_ref, kn_ref, vn_ref, kp_ref, vp_ref, gd_ref, sg_ref, lam_ref, o_ref):
    ts = q_ref.shape[1]
    lam = _lambda_value(lam_ref)
    lane = lax.broadcasted_iota(jnp.int32, (ts, LANES), 1)
    qrow = past + lax.broadcasted_iota(jnp.int32, (ts, past), 0)
    vis_p = (lax.broadcasted_iota(jnp.int32, (ts, past), 1) >> 6) <= (qrow >> 6)
    qrow_n = past + lax.broadcasted_iota(jnp.int32, (ts, ts), 0)
    vis_n = ((past + lax.broadcasted_iota(jnp.int32, (ts, ts), 1)) >> 6) <= (qrow_n >> 6)
    for h in range(DA_HEADS):
        sl = slice(h * LANES, (h + 1) * LANES)
        q = q_ref[0, :, sl]
        kp_t = kp_ref[0, sl, :].astype(BF16)
        kn = kn_ref[0, :, sl].astype(BF16)
        zero = jnp.zeros_like(q)
        wp, wn = [], []
        for c in range(2):
            qc = jnp.where((lane >= DA_HEAD_DIM) if c else (lane < DA_HEAD_DIM), q, zero)
            sp = jnp.where(vis_p, _dot(qc, kp_t), NEG_BIG)
            sn = jnp.where(vis_n, _dot_nt(qc, kn), NEG_BIG)
            m = jnp.maximum(jnp.max(sp, axis=-1, keepdims=True), jnp.max(sn, axis=-1, keepdims=True))
            pp = jnp.exp2(sp - m)
            pn = jnp.exp2(sn - m)
            inv = 1.0 / (jnp.sum(pp, axis=-1, keepdims=True) + jnp.sum(pn, axis=-1, keepdims=True))
            wp.append(pp * inv)
            wn.append(pn * inv)
        wgt_p = (wp[0] - lam * wp[1]).astype(BF16)
        wgt_n = (wn[0] - lam * wn[1]).astype(BF16)
        v_past = vp_ref[0, pl.ds(h, past, stride=DA_HEADS), :].astype(BF16)
        v_new = vn_ref[0, pl.ds(h, ts, stride=DA_HEADS), :].astype(BF16)
        o = _dot(wgt_p, v_past) + _dot(wgt_n, v_new)
        o = o * lax.rsqrt(jnp.mean(o * o, axis=-1, keepdims=True) + EPS)
        o = o * (sg_ref[...] * (1.0 - LAMBDA_INIT))
        o_ref[0, :, sl] = (o * gd_ref[0, :, sl].astype(F32)).astype(BF16)


def _attn_sample(q, k_new, v_new, k_past_t, v_past, gd, subln_g, lam_params):
    b, ts, _ = q.shape
    past = k_past_t.shape[2]
    new = pl.BlockSpec((1, ts, DA_WIDTH), lambda i: (i, 0, 0))
    new_v = pl.BlockSpec((1, ts * DA_HEADS, DA_VDIM), lambda i: (i, 0, 0))
    old_kt = pl.BlockSpec((1, DA_QK, past), lambda i: (i, 0, 0))
    old_v = pl.BlockSpec((1, past * DA_HEADS, DA_VDIM), lambda i: (i, 0, 0))
    return pl.pallas_call(
        functools.partial(_attn_sample_kernel, past),
        grid=(b,),
        in_specs=[new, new, new_v, old_kt, old_v, new,
                  pl.BlockSpec((1, DA_VDIM), lambda i: (0, 0)),
                  pl.BlockSpec((4, DA_HEAD_DIM), lambda i: (0, 0))],
        out_specs=new,
        out_shape=jax.ShapeDtypeStruct((b, ts, DA_WIDTH), BF16),
        compiler_params=pltpu.CompilerParams(
            dimension_semantics=("arbitrary",), vmem_limit_bytes=VMEM_LIMIT_BYTES),
        name="diff_attn_sample",
    )(q, k_new, v_new, k_past_t, v_past, gd, subln_g, lam_params)


def _out_proj_kernel(x_ref, a_ref, mix_ref, w_ref, y_ref):
    y_ref[0] = (x_ref[0] + _dot(a_ref[0], w_ref[:DA_WIDTH, :]) + _dot(mix_ref[0], w_ref[DA_WIDTH:, :]))


def _out_proj(x, out_a, mix, w_out, tm):
    b, t, _ = x.shape
    tile = lambda width: pl.BlockSpec((1, tm, width), lambda i, j: (i, j, 0))
    return pl.pallas_call(
        _out_proj_kernel,
        grid=(b, t // tm),
        in_specs=[tile(D_MODEL), tile(DA_WIDTH), tile(DA_WIDTH),
                  pl.BlockSpec((D_MODEL, D_MODEL), lambda i, j: (0, 0))],
        out_specs=tile(D_MODEL),
        out_shape=jax.ShapeDtypeStruct((b, t, D_MODEL), F32),
        compiler_params=pltpu.CompilerParams(
            dimension_semantics=("arbitrary", "arbitrary"), vmem_limit_bytes=VMEM_LIMIT_BYTES),
        name="out_proj",
    )(x, out_a, mix, w_out)


def _rope_tables(pos):
    half = ROT_DIM // 2
    inv = ROPE_THETA ** (-jnp.arange(0, ROT_DIM, 2, dtype=F32) / ROT_DIM)
    ang = pos.astype(F32)[:, None] * inv[None, :]
    cos, sin = jnp.cos(ang), jnp.sin(ang)
    t = pos.shape[0]
    ones = jnp.ones((t, DA_HEAD_DIM - ROT_DIM), F32)
    zeros = jnp.zeros((t, DA_HEAD_DIM - ROT_DIM), F32)
    zh = jnp.zeros((t, half), F32)
    c = jnp.concatenate([cos, cos, ones], axis=1)
    s_up = jnp.concatenate([-sin, zh, zeros], axis=1)
    s_dn = jnp.concatenate([zh, sin, zeros], axis=1)
    rep = LANES // DA_HEAD_DIM
    return tuple(jnp.tile(a, (1, rep)) for a in (c, s_up, s_dn))


def _block_diag(w):
    n, bi, bj = w.shape
    eye = jnp.eye(n, dtype=w.dtype)
    return (eye[:, None, :, None] * w[:, :, None, :]).reshape(n * bi, n * bj)


def _pick_tile(t, pref):
    return pref if t % pref == 0 else t


def kernel(x_prompt, x_sample, mem_prompt, cache_diff_k, cache_diff_v, cache_mem_k, cache_mem_v,
           state_lru_conv, state_lru_h, norm_g, w_in, da_q_norm_g, da_k_norm_g, lambda_q1, lambda_k1,
           lambda_q2, lambda_k2, da_subln_g, lru_conv_w, lru_conv_b, lru_w_a, lru_b_a, lru_w_x, lru_b_x,
           lru_lambda, mem_norm_g, w_mem_kv, mx_q_norm_g, mx_k_norm_g, w_out):
    depth = w_in.shape[0]
    assert depth == 1
    bp, tp, _ = x_prompt.shape
    bs, ts, _ = x_sample.shape
    past = cache_diff_k.shape[2]
    l = 0
    grp = np.arange(MXU_TILE) // DA_HEAD_DIM
    ones_bd = jnp.asarray(grp[:, None] == grp[None, :], BF16)
    wts = dict(
        w_in=w_in[l].astype(BF16), norm_g=norm_g[l][None, :],
        gq=jnp.tile(da_q_norm_g[l], DA_QK // DA_HEAD_DIM)[None, :],
        gk=jnp.tile(da_k_norm_g[l], DA_QK // DA_HEAD_DIM)[None, :],
        gmq=jnp.tile(mx_q_norm_g[l], MX_HEADS)[None, :], ones_bd=ones_bd,
        conv_w=lru_conv_w[l], conv_b=lru_conv_b[l][None, :],
        wa=_block_diag(lru_w_a[l]).astype(BF16), ba=lru_b_a[l][None, :],
        wx=_block_diag(lru_w_x[l]).astype(BF16), bx=lru_b_x[l][None, :],
        lru_lambda=lru_lambda[l][None, :])
    lam_params = jnp.stack([lambda_q1[l], lambda_k1[l], lambda_q2[l], lambda_k2[l]])
    subln_g = da_subln_g[l][None, :]
    w_out_b = w_out[l].astype(BF16)

    mk_p, mv_p = _mem_kv(mem_prompt, mem_norm_g[l][None, :], w_mem_kv[l].astype(BF16),
                         jnp.tile(mx_k_norm_g[l], MX_HEADS)[None, :], ones_bd)
    tm_p = _pick_tile(tp, 512)
    tq_p = _pick_tile(tp, 512)
    pos_p = jnp.arange(tp, dtype=jnp.int32)
    q_p, k_p, v_p, gd_p, mix_p, c_p, h_p, kb_p, vt_p = _in_proj(
        x_prompt, jnp.zeros((bp, CONV_W - 1, LRU_WIDTH), F32), jnp.zeros((bp, 1, LRU_WIDTH), F32),
        mk_p, mv_p, wts, _rope_tables(pos_p), tm_p, True)
    y_p = _attn_prompt(q_p, kb_p, vt_p, gd_p, subln_g, lam_params, x_prompt, mix_p, w_out_b,
                       tq_p, tm_p, ATTN_HEADS_PER_STEP)

    pos_s = past + jnp.arange(ts, dtype=jnp.int32)
    q_s, k_s, v_s, gd_s, mix_s, c_s, h_s = _in_proj(
        x_sample, state_lru_conv[l], state_lru_h[l][:, None, :],
        jnp.swapaxes(cache_mem_k[l].reshape(bs, N_MEM, MX_WIDTH), 1, 2),
        jnp.swapaxes(cache_mem_v[l].reshape(bs, N_MEM, MX_WIDTH), 1, 2),
        wts, _rope_tables(pos_s), ts, False)
    oa_s = _attn_sample(q_s, k_s, v_s, jnp.swapaxes(cache_diff_k[l].reshape(bs, past, DA_QK), 1, 2),
                        cache_diff_v[l].reshape(bs, past * DA_HEADS, DA_VDIM), gd_s, subln_g, lam_params)
    y_s = _out_proj(x_sample, oa_s, mix_s, w_out_b, ts)

    return (y_p, y_s,
            k_p.reshape(1, bp, tp, DA_HEADS, 2, DA_HEAD_DIM), v_p.reshape(1, bp, tp, DA_HEADS, DA_VDIM),
            jnp.swapaxes(mk_p, 1, 2).reshape(1, bp, N_MEM, MX_HEADS, MX_HEAD_DIM),
            jnp.swapaxes(mv_p, 1, 2).reshape(1, bp, N_MEM, MX_HEADS, MX_HEAD_DIM),
            c_p[None], h_p.reshape(1, bp, LRU_WIDTH),
            k_s.reshape(1, bs, ts, DA_HEADS, 2, DA_HEAD_DIM), v_s.reshape(1, bs, ts, DA_HEADS, DA_VDIM),
            c_s[None], h_s.reshape(1, bs, LRU_WIDTH))
```
